```python
import math
import jax, jax.numpy as jnp
from jax import lax
import numpy as np

D_MODEL = 1024
BATCH = 8
SEQ = 8192
DEPTH = 1
DEC_BATCH = 128
DEC_SEQ = 4
PAST_LEN = 8192
PAGE_SIZE = 128

HEAD_DIM = 64
NSA_HEADS = 8
NSA_KV = 2
NSA_REP = NSA_HEADS // NSA_KV
CMP_BLOCK = 32
CMP_STRIDE = 16
CMP_HIDDEN = 128
SEL_BLOCK = 64
SEL_TOPK = 16
WINDOW = 512
DIFF_HEADS = 4
DIFF_QK = 64
DIFF_V = 2 * DIFF_QK
D_FF = 4 * D_MODEL
PLE_DIM = 256
ROPE_THETA = 10000.0
EPS = 1e-6
Q_BLOCK = 128
NEG = -1e30
FORCED = 1e9
NSA_Q_W = NSA_HEADS * HEAD_DIM
NSA_KV_W = NSA_KV * HEAD_DIM
NSA_GATE_W = NSA_HEADS * 3
DIFF_Q_W = DIFF_HEADS * 2 * DIFF_QK
DIFF_K_W = DIFF_HEADS * 2 * DIFF_QK
DIFF_V_W = DIFF_HEADS * DIFF_V
PROJ_W = NSA_Q_W + 6 * NSA_KV_W + NSA_GATE_W + DIFF_Q_W + DIFF_K_W + DIFF_V_W
MIX_W = NSA_Q_W + DIFF_V_W

kernel_name = 'hybrid_nsa_diffattn_step'


def rms_norm(x, g):
    xf = x.astype(jnp.float32)
    y = xf * lax.rsqrt(jnp.mean(xf * xf, axis=-1, keepdims=True) + EPS)
    return (y * g.astype(jnp.float32)).astype(x.dtype)


def rope(x, pos):
    half = x.shape[-1] // 2
    inv = ROPE_THETA ** (-jnp.arange(half, dtype=jnp.float32) / half)
    ang = pos.astype(jnp.float32)[:, None] * inv[None, :]
    cos = jnp.cos(ang)[:, None, :]
    sin = jnp.sin(ang)[:, None, :]
    xf = x.astype(jnp.float32)
    x1, x2 = xf[..., :half], xf[..., half:]
    return jnp.concatenate([x1 * cos - x2 * sin, x2 * cos + x1 * sin], axis=-1).astype(x.dtype)


def masked_softmax(s, mask):
    p = jax.nn.softmax(jnp.where(mask, s, NEG), axis=-1)
    return jnp.where(mask, p, 0.0)


def compress(k, pos_emb, w1, w2):
    n = (k.shape[0] - CMP_BLOCK) // CMP_STRIDE + 1
    idx = jnp.arange(n)[:, None] * CMP_STRIDE + jnp.arange(CMP_BLOCK)[None, :]
    blk = k[idx] + pos_emb[None, :, None, :]
    blk = blk.transpose(0, 2, 1, 3).reshape(n, k.shape[1], CMP_BLOCK * HEAD_DIM)
    return jax.nn.gelu(blk @ w1) @ w2


def cmp_ends(n_cmp):
    return jnp.arange(n_cmp) * CMP_STRIDE + CMP_BLOCK - 1


def sel_overlap(n_cmp, n_sel):
    cs = jnp.arange(n_cmp)[:, None] * CMP_STRIDE
    bs = jnp.arange(n_sel)[None, :] * SEL_BLOCK
    ov = jnp.clip(jnp.minimum(cs + CMP_BLOCK, bs + SEL_BLOCK) - jnp.maximum(cs, bs), 0, None)
    return ov.astype(jnp.float32) / CMP_BLOCK


def nsa_seq_keys(rows, cmp_pos, wk1, wk2, wv1, wv2):
    L = rows.shape[0]
    kc = compress(rows[:, 0], cmp_pos, wk1, wk2)
    vc = compress(rows[:, 1], cmp_pos, wv1, wv2)
    n_sel = -(-L // SEL_BLOCK)
    sel = jnp.pad(rows[:, 2:4], ((0, n_sel * SEL_BLOCK - L), (0, 0), (0, 0), (0, 0)))
    sel = sel.reshape(n_sel, SEL_BLOCK, 2, NSA_KV, HEAD_DIM)
    return kc, vc, sel[:, :, 0], sel[:, :, 1]


def nsa_attend(q, q_rot, gates, q_pos, kc, vc, c_end, ks, vs, kw, vw, w_pos, ovl):
    f32 = jnp.float32
    nq = q.shape[0]
    scale = HEAD_DIM ** -0.5
    qg = q.reshape(nq, NSA_KV, NSA_REP, HEAD_DIM)
    qr = q_rot.reshape(nq, NSA_KV, NSA_REP, HEAD_DIM)
    t = q_pos[:, None]
    s_c = jnp.einsum('qgrd,cgd->qgrc', qg, kc).astype(f32) * scale
    p_c = masked_softmax(s_c, (c_end[None, :] <= t)[:, None, None, :])
    o_c = jnp.einsum('qgrc,cgd->qgrd', p_c.astype(vc.dtype), vc)
    n_sel = ks.shape[0]
    imp = jnp.einsum('qgrc,cj->qgj', p_c, ovl)
    j = jnp.arange(n_sel)[None, None, :]
    cur = (q_pos // SEL_BLOCK)[:, None, None]
    forced = (j == 0) | (j == cur) | (j == cur - 1)
    imp = jnp.where(j > cur, NEG, jnp.where(forced, FORCED, imp))
    vals, idx = lax.top_k(imp, min(SEL_TOPK, n_sel))
    gi = jnp.arange(NSA_KV)[None, :, None]
    k_sel = ks.transpose(2, 0, 1, 3)[gi, idx]
    v_sel = vs.transpose(2, 0, 1, 3)[gi, idx]
    n_k = idx.shape[-1] * SEL_BLOCK
    s_pos = idx[..., None] * SEL_BLOCK + jnp.arange(SEL_BLOCK)
    m_s = (vals[..., None] > 0.5 * NEG) & (s_pos <= q_pos[:, None, None, None])
    s_s = jnp.einsum('qgrd,qgksd->qgrks', qr, k_sel).astype(f32) * scale
    p_s = masked_softmax(s_s.reshape(nq, NSA_KV, NSA_REP, n_k), m_s.reshape(nq, NSA_KV, 1, n_k))
    o_s = jnp.einsum('qgrn,qgnd->qgrd', p_s.astype(vs.dtype), v_sel.reshape(nq, NSA_KV, n_k, HEAD_DIM))
    dist = t - w_pos[None, :]
    m_w = ((dist >= 0) & (dist < WINDOW) & (w_pos[None, :] >= 0))[:, None, None, :]
    s_w = jnp.einsum('qgrd,lgd->qgrl', qr, kw).astype(f32) * scale
    p_w = masked_softmax(s_w, m_w)
    o_w = jnp.einsum('qgrl,lgd->qgrd', p_w.astype(vw.dtype), vw)
    g = gates.reshape(nq, NSA_KV, NSA_REP, 3)
    out = g[..., 0:1] * o_c + g[..., 1:2] * o_s + g[..., 2:3] * o_w
    return out.reshape(nq, NSA_HEADS, HEAD_DIM)


def diff_attend(q1, q2, q_pos, k1, k2, v, k_pos, lam):
    f32 = jnp.float32
    scale = DIFF_QK ** -0.5
    m = (k_pos[None, :] <= q_pos[:, None])[None]
    a1 = masked_softmax(jnp.einsum('qhd,khd->hqk', q1, k1).astype(f32) * scale, m)
    a2 = masked_softmax(jnp.einsum('qhd,khd->hqk', q2, k2).astype(f32) * scale, m)
    a = a1 - lam * a2
    return jnp.einsum('hqk,khd->qhd', a.astype(v.dtype), v)


def project(h, pos, w_in):
    n, l = h.shape[0], h.shape[1]
    z = h @ w_in
    o1 = NSA_Q_W
    o2 = o1 + 6 * NSA_KV_W
    o3 = o2 + NSA_GATE_W
    o4 = o3 + DIFF_Q_W
    o5 = o4 + DIFF_K_W
    q = z[..., :o1].reshape(n, l, NSA_HEADS, HEAD_DIM)
    kv = z[..., o1:o2].reshape(n, l, 6, NSA_KV, HEAD_DIM)
    gates = jax.nn.sigmoid(z[..., o2:o3].reshape(n, l, NSA_HEADS, 3))
    dq = z[..., o3:o4].reshape(n, l, DIFF_HEADS, 2, DIFF_QK)
    dk = z[..., o4:o5].reshape(n, l, DIFF_HEADS, 2, DIFF_QK)
    dv = z[..., o5:].reshape(n, l, DIFF_HEADS, DIFF_V)
    q_rot = rope(q, pos)
    nsa_rows = jnp.stack([kv[:, :, 0], kv[:, :, 1], rope(kv[:, :, 2], pos), kv[:, :, 3]], axis=2)
    win_rows = jnp.stack([rope(kv[:, :, 4], pos), kv[:, :, 5]], axis=2)
    dq1 = rope(dq[..., 0, :], pos)
    dq2 = rope(dq[..., 1, :], pos)
    dk_rot = jnp.concatenate([rope(dk[..., 0, :], pos), rope(dk[..., 1, :], pos)], axis=-1)
    diff_rows = jnp.stack([dk_rot, dv], axis=2)
    return q, q_rot, gates, dq1, dq2, nsa_rows, win_rows, diff_rows


def merge_heads(o_n, o_d, d_gain, lam_init, w_out, g_post):
    n, l = o_n.shape[0], o_n.shape[1]
    o_d = rms_norm(o_d, d_gain) * (1.0 - lam_init)
    cat = jnp.concatenate([o_n.reshape(n, l, NSA_Q_W), o_d.reshape(n, l, DIFF_V_W)], axis=-1)
    return rms_norm(cat @ w_out, g_post)


def channel_and_ple(x, p, g_pre, w_up, w_down, g_post, w_ple, w_gate, g_ple):
    h = rms_norm(x, g_pre)
    f = jnp.square(jax.nn.relu(h @ w_up)) @ w_down
    x = x + rms_norm(f, g_post)
    e = (p @ w_ple) * jax.nn.sigmoid(x @ w_gate)
    return x + rms_norm(e, g_ple)


def setup_inputs(seed: int = 0) -> dict:
    key = jax.random.key(seed)
    k = jax.random.split(key, 32)
    f32 = jnp.float32
    n_pages = PAST_LEN // PAGE_SIZE
    used = DEC_BATCH * n_pages
    n_pool = used + used // 4
    w_buf = min(WINDOW, PAST_LEN)

    def nrm(kk, shape, scale=1.0):
        return jax.random.normal(kk, shape, f32) * scale

    def gain(kk, n):
        return 1.0 + 0.02 * jax.random.normal(kk, (DEPTH, n), f32)

    page_table = jax.random.permutation(k[5], n_pool)[:used].reshape(DEC_BATCH, n_pages).astype(jnp.int32)
    cw = CMP_BLOCK * HEAD_DIM
    return {
        'x_prompt': nrm(k[0], (BATCH, SEQ, D_MODEL)),
        'x_sample': nrm(k[1], (DEC_BATCH, DEC_SEQ, D_MODEL)),
        'cache_nsa': nrm(k[2], (DEPTH, n_pool, PAGE_SIZE, 4, NSA_KV, HEAD_DIM)),
        'cache_diff': nrm(k[3], (DEPTH, n_pool, PAGE_SIZE, 2, DIFF_HEADS, DIFF_V)),
        'state_nsa_win': nrm(k[4], (DEPTH, DEC_BATCH, w_buf, 2, NSA_KV, HEAD_DIM)),
        'page_table': page_table,
        'p_prompt': nrm(k[6], (DEPTH, BATCH, SEQ, PLE_DIM)),
        'p_sample': nrm(k[7], (DEPTH, DEC_BATCH, DEC_SEQ, PLE_DIM)),
        'g_mix_pre': gain(k[8], D_MODEL),
        'w_in': nrm(k[9], (DEPTH, D_MODEL, PROJ_W), D_MODEL ** -0.5),
        'cmp_pos': nrm(k[10], (DEPTH, CMP_BLOCK, HEAD_DIM), 0.1),
        'cmp_k_w1': nrm(k[11], (DEPTH, cw, CMP_HIDDEN), cw ** -0.5),
        'cmp_k_w2': nrm(k[12], (DEPTH, CMP_HIDDEN, HEAD_DIM), CMP_HIDDEN ** -0.5),
        'cmp_v_w1': nrm(k[13], (DEPTH, cw, CMP_HIDDEN), cw ** -0.5),
        'cmp_v_w2': nrm(k[14], (DEPTH, CMP_HIDDEN, HEAD_DIM), CMP_HIDDEN ** -0.5),
        'diff_lq1': nrm(k[15], (DEPTH, DIFF_QK), 0.1),
        'diff_lk1': nrm(k[16], (DEPTH, DIFF_QK), 0.1),
        'diff_lq2': nrm(k[17], (DEPTH, DIFF_QK), 0.1),
        'diff_lk2': nrm(k[18], (DEPTH, DIFF_QK), 0.1),
        'diff_gain': gain(k[19], DIFF_V),
        'w_out': nrm(k[20], (DEPTH, MIX_W, D_MODEL), MIX_W ** -0.5),
        'g_mix_post': gain(k[21], D_MODEL),
        'g_ffn_pre': gain(k[22], D_MODEL),
        'w_up': nrm(k[23], (DEPTH, D_MODEL, D_FF), D_MODEL ** -0.5),
        'w_down': nrm(k[24], (DEPTH, D_FF, D_MODEL), D_FF ** -0.5),
        'g_ffn_post': gain(k[25], D_MODEL),
        'w_ple': nrm(k[26], (DEPTH, PLE_DIM, D_MODEL), PLE_DIM ** -0.5),
        'w_ple_gate': nrm(k[27], (DEPTH, D_MODEL, D_MODEL), D_MODEL ** -0.5),
        'g_ple': gain(k[28], D_MODEL),
    }


def reference(x_prompt, x_sample, cache_nsa, cache_diff, state_nsa_win, page_table, p_prompt, p_sample,
              g_mix_pre, w_in, cmp_pos, cmp_k_w1, cmp_k_w2, cmp_v_w1, cmp_v_w2,
              diff_lq1, diff_lk1, diff_lq2, diff_lk2, diff_gain, w_out, g_mix_post,
              g_ffn_pre, w_up, w_down, g_ffn_post, w_ple, w_ple_gate, g_ple):
    f32 = jnp.float32
    B, S = x_prompt.shape[0], x_prompt.shape[1]
    Sd = x_sample.shape[1]
    past_len = page_table.shape[1] * cache_nsa.shape[2]
    L_s = past_len + Sd
    pos_p = jnp.arange(S)
    pos_s = past_len + jnp.arange(Sd)
    qblk = min(Q_BLOCK, S)
    nqb = S // qblk
    wb = state_nsa_win.shape[2]
    xp, xs = x_prompt, x_sample
    nsa_p, nsa_s, diff_p, diff_s, win_p, win_s = [], [], [], [], [], []
    for i in range(DEPTH):
        lam_init = 0.8 - 0.6 * math.exp(-0.3 * i)
        lam = (jnp.exp(jnp.sum(diff_lq1[i].astype(f32) * diff_lk1[i].astype(f32)))
               - jnp.exp(jnp.sum(diff_lq2[i].astype(f32) * diff_lk2[i].astype(f32))) + lam_init)

        hp = rms_norm(xp, g_mix_pre[i])
        q, q_rot, gates, dq1, dq2, nsa_rows, win_rows, diff_rows = project(hp, pos_p, w_in[i])
        kc, vc, ks_, vs_ = jax.vmap(nsa_seq_keys, in_axes=(0, None, None, None, None, None))(
            nsa_rows, cmp_pos[i], cmp_k_w1[i], cmp_k_w2[i], cmp_v_w1[i], cmp_v_w2[i])
        c_end = cmp_ends(kc.shape[1])
        ovl = sel_overlap(kc.shape[1], ks_.shape[1])
        kw_pad = jnp.pad(win_rows, ((0, 0), (WINDOW, 0), (0, 0), (0, 0), (0, 0)))
        dk1 = diff_rows[:, :, 0, :, :DIFF_QK]
        dk2 = diff_rows[:, :, 0, :, DIFF_QK:]
        dvv = diff_rows[:, :, 1]

        def blocks(a):
            return a.reshape((B * nqb, qblk) + a.shape[2:])

        b_ids = jnp.repeat(jnp.arange(B), nqb)
        qb_ids = jnp.tile(jnp.arange(nqb), B)

        def prompt_block(item):
            qq, qqr, gg, q1, q2, b, qb = item
            qpos = qb * qblk + jnp.arange(qblk)
            win = lax.dynamic_slice_in_dim(kw_pad[b], qb * qblk, WINDOW + qblk, axis=0)
            wpos = qb * qblk - WINDOW + jnp.arange(WINDOW + qblk)
            o_n = nsa_attend(qq, qqr, gg, qpos, kc[b], vc[b], c_end, ks_[b], vs_[b],
                             win[:, 0], win[:, 1], wpos, ovl)
            o_d = diff_attend(q1, q2, qpos, dk1[b], dk2[b], dvv[b], pos_p, lam)
            return o_n, o_d

        o_n, o_d = lax.map(prompt_block, (blocks(q), blocks(q_rot), blocks(gates), blocks(dq1), blocks(dq2),
                                          b_ids, qb_ids))
        o_n = o_n.reshape(B, S, NSA_HEADS, HEAD_DIM)
        o_d = o_d.reshape(B, S, DIFF_HEADS, DIFF_V)
        xp = xp + merge_heads(o_n, o_d, diff_gain[i], lam_init, w_out[i], g_mix_post[i])
        xp = channel_and_ple(xp, p_prompt[i], g_ffn_pre[i], w_up[i], w_down[i], g_ffn_post[i],
                             w_ple[i], w_ple_gate[i], g_ple[i])
        nsa_p.append(nsa_rows)
        diff_p.append(diff_rows)
        win_p.append(win_rows[:, S - min(WINDOW, S):])

        hs = rms_norm(xs, g_mix_pre[i])
        q, q_rot, gates, dq1, dq2, nsa_new, win_new, diff_new = project(hs, pos_s, w_in[i])
        nc_s = (L_s - CMP_BLOCK) // CMP_STRIDE + 1
        c_end_s = cmp_ends(nc_s)
        ovl_s = sel_overlap(nc_s, -(-L_s // SEL_BLOCK))
        win_all = jnp.concatenate([state_nsa_win[i], win_new], axis=1)
        wpos_s = past_len - wb + jnp.arange(wb + Sd)
        kpos_s = jnp.arange(L_s)

        def sample_seq(item):
            pt, qq, qqr, gg, q1, q2, nn, dn, win = item
            rows = jnp.concatenate([cache_nsa[i, pt].reshape((past_len,) + cache_nsa.shape[3:]), nn], axis=0)
            kc1, vc1, ks1, vs1 = nsa_seq_keys(rows, cmp_pos[i], cmp_k_w1[i], cmp_k_w2[i], cmp_v_w1[i], cmp_v_w2[i])
            o_n1 = nsa_attend(qq, qqr, gg, pos_s, kc1, vc1, c_end_s, ks1, vs1, win[:, 0], win[:, 1], wpos_s, ovl_s)
            drows = jnp.concatenate([cache_diff[i, pt].reshape((past_len,) + cache_diff.shape[3:]), dn], axis=0)
            o_d1 = diff_attend(q1, q2, pos_s, drows[:, 0, :, :DIFF_QK], drows[:, 0, :, DIFF_QK:], drows[:, 1],
                               kpos_s, lam)
            return o_n1, o_d1

        o_n, o_d = lax.map(sample_seq, (page_table, q, q_rot, gates, dq1, dq2, nsa_new, diff_new, win_all))
        xs = xs + merge_heads(o_n, o_d, diff_gain[i], lam_init, w_out[i], g_mix_post[i])
        xs = channel_and_ple(xs, p_sample[i], g_ffn_pre[i], w_up[i], w_down[i], g_ffn_post[i],
                             w_ple[i], w_ple_gate[i], g_ple[i])
        nsa_s.append(nsa_new)
        diff_s.append(diff_new)
        win_s.append(win_all[:, win_all.shape[1] - min(WINDOW, L_s):])

    return (xp, xs, jnp.stack(nsa_p), jnp.stack(nsa_s), jnp.stack(diff_p), jnp.stack(diff_s),
            jnp.stack(win_p), jnp.stack(win_s))
```

```python
import functools
import math

import numpy as np
import jax
import jax.numpy as jnp
from jax import lax
from jax.experimental import pallas as pl
from jax.experimental.pallas import tpu as pltpu

F32 = jnp.float32
BF16 = jnp.bfloat16

HEAD_DIM = 64
NSA_HEADS = 8
NSA_KV = 2
NSA_REP = NSA_HEADS // NSA_KV
CMP_BLOCK = 32
CMP_STRIDE = 16
CMP_HIDDEN = 128
SEL_BLOCK = 64
SEL_TOPK = 16
WINDOW = 512
DIFF_HEADS = 4
DIFF_QK = 64
DIFF_V = 2 * DIFF_QK
ROPE_THETA = 10000.0
EPS = 1e-6
NEG = -1e30
FORCED = 1e9
ATT_SCALE = HEAD_DIM ** -0.5

V7X_LANES = 128
V7X_SUBLANES = 8
V7X_VMEM_BYTES = 64 * 1024 * 1024
V7X_VMEM_USABLE = V7X_VMEM_BYTES - 8 * 1024 * 1024

DEC_ROWS = V7X_SUBLANES
Q_TILE = 128
K_TILE = 512
TOK_TILE = 512

_C_Q = 0
_C_KV = 512
_C_DQ = 1280
_C_DK = 1792
_C_DV = 2304
_C_GATE = 2816
_C_END = 2944


def _cparams(sem, vmem_bytes):
    return pltpu.CompilerParams(dimension_semantics=sem,
                                vmem_limit_bytes=int(min(max(vmem_bytes, 16 << 20), V7X_VMEM_USABLE)))


def _dot(a, b):
    return jnp.dot(a, b, preferred_element_type=F32)


def _dot_nt(a, b):
    return lax.dot_general(a, b, (((1,), (1,)), ((), ())), preferred_element_type=F32)


def _rms(x, g):
    return x * lax.rsqrt(jnp.mean(x * x, axis=-1, keepdims=True) + EPS) * g


def _proj_kernel(x_ref, g_ref, w_ref, cos_ref, sin_ref, *outs, tm, nblk_tab, with_kv):
    qc_ref, qr_ref, dq_ref, gate_ref, nsa_ref, win_ref, diff_ref = outs[:7]
    x = x_ref[...]
    h = _rms(x, g_ref[...]).astype(BF16)
    cos = cos_ref[...]
    sin = sin_ref[...]
    lane = lax.broadcasted_iota(jnp.int32, (tm, V7X_LANES), 1)
    low = lane < HEAD_DIM
    first = (lane & (HEAD_DIM - 1)) < (HEAD_DIM // 2)

    def rope(zc):
        partner = jnp.where(first, pltpu.roll(zc, 96, 1), pltpu.roll(zc, 32, 1))
        return zc * cos + partner * sin

    def pad_heads(zc):
        return jnp.where(low, zc, 0.0), jnp.where(low, pltpu.roll(zc, 64, 1), 0.0)

    def mm(a, b):
        return _dot(h, w_ref[:, a:b])

    zq = mm(_C_Q, _C_KV)
    for c in range(4):
        zc = zq[:, c * 128:(c + 1) * 128]
        e, o = pad_heads(zc * ATT_SCALE)
        qc_ref[:, (2 * c) * 128:(2 * c + 1) * 128] = e.astype(BF16)
        qc_ref[:, (2 * c + 1) * 128:(2 * c + 2) * 128] = o.astype(BF16)
        e, o = pad_heads(rope(zc) * ATT_SCALE)
        qr_ref[:, (2 * c) * 128:(2 * c + 1) * 128] = e.astype(BF16)
        qr_ref[:, (2 * c + 1) * 128:(2 * c + 2) * 128] = o.astype(BF16)

    zkv = mm(_C_KV, _C_DQ)
    nsa_ref[:, 0:256] = zkv[:, 0:256]
    k2 = rope(zkv[:, 256:384])
    nsa_ref[:, 256:384] = k2
    v3 = zkv[:, 384:512]
    nsa_ref[:, 384:512] = v3
    k4 = rope(zkv[:, 512:640])
    win_ref[:, 0:128] = k4
    v5 = zkv[:, 640:768]
    win_ref[:, 128:256] = v5

    zdq = mm(_C_DQ, _C_DK)
    for c in range(4):
        dq_ref[:, c * 128:(c + 1) * 128] = (rope(zdq[:, c * 128:(c + 1) * 128]) * ATT_SCALE).astype(BF16)

    zdk = mm(_C_DK, _C_DV)
    dk_rot = [rope(zdk[:, c * 128:(c + 1) * 128]) for c in range(4)]
    for c in range(4):
        diff_ref[:, c * 128:(c + 1) * 128] = dk_rot[c]
    zdv = mm(_C_DV, _C_GATE)
    diff_ref[:, 512:1024] = zdv

    gate_ref[...] = jax.nn.sigmoid(mm(_C_GATE, _C_END))

    if with_kv:
        kaug_ref, vslcT_ref, kwin_ref, vwinT_ref, dk_ref, dvT_ref = outs[7:]
        base = (pl.program_id(0) % nblk_tab) * tm
        pos = base + lax.broadcasted_iota(jnp.int32, (tm, V7X_LANES), 0)
        onehot = jnp.where(lane == (pos >> 6), 1.0, 0.0).astype(BF16)
        e, o = pad_heads(k2)
        kaug_ref[:, 0:128] = e.astype(BF16)
        kaug_ref[:, 128:256] = onehot
        kaug_ref[:, 256:384] = o.astype(BF16)
        kaug_ref[:, 384:512] = onehot
        vslcT_ref[...] = v3.T.astype(BF16)
        e, o = pad_heads(k4)
        kwin_ref[:, 0:128] = e.astype(BF16)
        kwin_ref[:, 128:256] = o.astype(BF16)
        v5t = v5.T.astype(BF16)
        for j in range(tm // 128):
            vwinT_ref[j] = v5t[:, j * 128:(j + 1) * 128]
        for c in range(4):
            dk_ref[:, c * 128:(c + 1) * 128] = dk_rot[c].astype(BF16)
            dvT_ref[c * 128:(c + 1) * 128, :] = zdv[:, c * 128:(c + 1) * 128].T.astype(BF16)


def _project(x2d, g, wcat, cos_tab, sin_tab, *, seq_len, with_kv):
    t, d = x2d.shape
    tm = min(TOK_TILE, t)
    assert t % tm == 0 and cos_tab.shape[0] % tm == 0
    nblk_tab = cos_tab.shape[0] // tm
    nblk = t // tm
    row = lambda w: pl.BlockSpec((tm, w), lambda i: (i, 0))
    out_shape = [
        jax.ShapeDtypeStruct((t, 1024), BF16),
        jax.ShapeDtypeStruct((t, 1024), BF16),
        jax.ShapeDtypeStruct((t, 512), BF16),
        jax.ShapeDtypeStruct((t, 128), F32),
        jax.ShapeDtypeStruct((t, 512), F32),
        jax.ShapeDtypeStruct((t, 256), F32),
        jax.ShapeDtypeStruct((t, 1024), F32),
    ]
    out_specs = [row(1024), row(1024), row(512), row(128), row(512), row(256), row(1024)]
    if with_kv:
        assert seq_len % tm == 0 and tm == K_TILE
        b = t // seq_len
        nt = seq_len // tm
        out_shape += [
            jax.ShapeDtypeStruct((t, 512), BF16),
            jax.ShapeDtypeStruct((b, nt, 128, tm), BF16),
            jax.ShapeDtypeStruct((t, 256), BF16),
            jax.ShapeDtypeStruct((b, seq_len // 128, 128, 128), BF16),
            jax.ShapeDtypeStruct((t, 512), BF16),
            jax.ShapeDtypeStruct((b, nt, 512, tm), BF16),
        ]
        out_specs += [
            row(512),
            pl.BlockSpec((None, None, 128, tm), lambda i: (i // nt, i % nt, 0, 0)),
            row(256),
            pl.BlockSpec((None, tm // 128, 128, 128), lambda i: (i // nt, i % nt, 0, 0)),
            row(512),
            pl.BlockSpec((None, None, 512, tm), lambda i: (i // nt, i % nt, 0, 0)),
        ]
    return pl.pallas_call(
        functools.partial(_proj_kernel, tm=tm, nblk_tab=nblk_tab, with_kv=with_kv),
        grid=(nblk,),
        in_specs=[
            row(d),
            pl.BlockSpec((1, d), lambda i: (0, 0)),
            pl.BlockSpec(wcat.shape, lambda i: (0, 0)),
            pl.BlockSpec((tm, 128), lambda i: (i % nblk_tab, 0)),
            pl.BlockSpec((tm, 128), lambda i: (i % nblk_tab, 0)),
        ],
        out_specs=out_specs,
        out_shape=out_shape,
        compiler_params=_cparams(("parallel",), 48 << 20),
        name="proj_kv" if with_kv else "proj",
    )(x2d, g, wcat, cos_tab, sin_tab)


def _pair_lhs(xa, xb, low):
    l0 = jnp.where(low, xa, pltpu.roll(xb, 64, 1))
    l1 = jnp.where(low, pltpu.roll(xa, 64, 1), xb)
    return jnp.concatenate([l0, l1], axis=0).astype(BF16)


def _compress_bias(pos_ref, wpair_ref, s):
    r = jnp.zeros((16, 2 * CMP_HIDDEN), F32)
    for p in range(8):
        lhs = jnp.concatenate([jnp.broadcast_to(pos_ref[0, p:p + 1, :], (8, 128)),
                               jnp.broadcast_to(pos_ref[1, p:p + 1, :], (8, 128))], axis=0).astype(BF16)
        r = r + _dot(lhs, wpair_ref[s, p])
    return r[0:1, 0:CMP_HIDDEN] + r[8:9, CMP_HIDDEN:]


def _compress_finalize(ab_ref, pos_ref, wpair_ref, w2bd_ref, nch):
    row = lax.broadcasted_iota(jnp.int32, (nch, CMP_HIDDEN), 0)
    outs = []
    for s in range(2):
        bias = _compress_bias(pos_ref, wpair_ref, s)
        hid = []
        for g in range(NSA_KV):
            a = ab_ref[s, g * nch:(g + 1) * nch, 0:CMP_HIDDEN]
            bsh = ab_ref[s, pl.ds(g * nch + 1, nch), CMP_HIDDEN:2 * CMP_HIDDEN]
            hg = jax.nn.gelu(a + bsh + bias)
            hid.append(jnp.where(row < nch - 1, hg, 0.0))
        outs.append(_dot(jnp.concatenate(hid, axis=1).astype(BF16), w2bd_ref[s]))
    return outs[0], outs[1]


def _compress_prompt_kernel(rows_k_ref, rows_v_ref, wpair_ref, pos_ref, w2bd_ref, kc2_ref, vcT_ref, ab_ref, *, nch):
    lane = lax.broadcasted_iota(jnp.int32, (nch, V7X_LANES), 1)
    low = lane < HEAD_DIM
    ab_ref[:, 2 * nch:2 * nch + 8, :] = jnp.zeros((2, 8, 2 * CMP_HIDDEN), F32)
    for s, rows_ref in enumerate((rows_k_ref, rows_v_ref)):
        acc = jnp.zeros((2 * nch, 2 * CMP_HIDDEN), F32)
        for p in range(8):
            xa = rows_ref[pl.ds(2 * p, nch, stride=CMP_STRIDE), :]
            xb = rows_ref[pl.ds(2 * p + 1, nch, stride=CMP_STRIDE), :]
            acc = acc + _dot(_pair_lhs(xa, xb, low), wpair_ref[s, p])
        ab_ref[s, 0:2 * nch, :] = acc
    kc, vc = _compress_finalize(ab_ref, pos_ref, wpair_ref, w2bd_ref, nch)
    kc2_ref[0] = kc.astype(BF16)
    kc2_ref[1] = pltpu.roll(kc, 64, 1).astype(BF16)
    vcT_ref[...] = vc.T.astype(BF16)


def _compress_prompt(nsa_rows2d, wpair, pospair, w2bd, *, b, s):
    nch = s // CMP_STRIDE
    return pl.pallas_call(
        functools.partial(_compress_prompt_kernel, nch=nch),
        grid=(b,),
        in_specs=[
            pl.BlockSpec((s, 128), lambda i: (i, 0)),
            pl.BlockSpec((s, 128), lambda i: (i, 1)),
            pl.BlockSpec(wpair.shape, lambda i: (0, 0, 0, 0)),
            pl.BlockSpec(pospair.shape, lambda i: (0, 0, 0)),
            pl.BlockSpec(w2bd.shape, lambda i: (0, 0, 0)),
        ],
        out_specs=[
            pl.BlockSpec((None, 2, nch, 128), lambda i: (i, 0, 0, 0)),
            pl.BlockSpec((None, 128, nch), lambda i: (i, 0, 0)),
        ],
        out_shape=[jax.ShapeDtypeStruct((b, 2, nch, 128), BF16),
                   jax.ShapeDtypeStruct((b, 128, nch), BF16)],
        scratch_shapes=[pltpu.VMEM((2, 2 * nch + 8, 2 * CMP_HIDDEN), F32)],
        compiler_params=_cparams(("parallel",), 40 << 20),
        name="compress_prompt",
    )(nsa_rows2d, nsa_rows2d, wpair, pospair, w2bd)


def _rank_blocks_T(e_ref, rank_ref, jmax):
    nv = 128 // 8
    srow = lax.broadcasted_iota(jnp.int32, (8, 128), 0)
    rank_ref[...] = jnp.zeros((128, 128), F32)
    for c in range(nv):
        @pl.when(8 * c <= jmax)
        def _():
            ec = e_ref[8 * c:8 * c + 8, :]
            rows = [jnp.broadcast_to(ec[r:r + 1, :], (8, 128)) for r in range(8)]
            for v in range(nv):
                ev = e_ref[8 * v:8 * v + 8, :]
                cnt = jnp.zeros((8, 128), F32)
                for r in range(8):
                    if v > c:
                        cnt = cnt + jnp.where(rows[r] >= ev, 1.0, 0.0)
                    elif v < c:
                        cnt = cnt + jnp.where(rows[r] > ev, 1.0, 0.0)
                    else:
                        cnt = cnt + jnp.where(srow > r, jnp.where(rows[r] >= ev, 1.0, 0.0),
                                              jnp.where(rows[r] > ev, 1.0, 0.0))
                rank_ref[8 * v:8 * v + 8, :] += cnt


def _softmax_cols(s, valid):
    s = jnp.where(valid, s, NEG)
    m = jnp.max(s, axis=0, keepdims=True)
    e = jnp.where(valid, jnp.exp(s - m), 0.0)
    l = jnp.sum(e, axis=0, keepdims=True)
    return e * (1.0 / jnp.where(l > 0.0, l, 1.0))


def _nsa_prompt_kernel(qc_ref, qr_ref, gate_ref, kc2_ref, vcT_ref, ovlT_ref, kaug_ref, vslcT_ref,
                       kwin_ref, vwinT_ref, o_ref, e_ref, rank_ref, *, tq, tk, nch):
    qb = pl.program_id(1)
    t0 = qb * tq
    ncol = NSA_REP * tq
    col = lax.broadcasted_iota(jnp.int32, (1, ncol), 1)
    tcol = t0 + (col & (tq - 1))
    cidx = lax.broadcasted_iota(jnp.int32, (nch, ncol), 0)
    cvalid = (cidx * CMP_STRIDE + (CMP_BLOCK - 1)) <= tcol
    jrow = lax.broadcasted_iota(jnp.int32, (128, tq), 0)
    cur = (t0 + lax.broadcasted_iota(jnp.int32, (128, tq), 1)) >> 6
    jmax = (t0 + tq - 1) >> 6
    nkt = (t0 + tq - 1) // tk + 1
    wstart = pl.multiple_of(jnp.maximum(qb - WINDOW // tq, 0) * tq, tq)
    nwk = WINDOW + tq
    gT = gate_ref[...].T

    for g in range(NSA_KV):
        heads = [g * NSA_REP + r for r in range(NSA_REP)]
        qc_g = jnp.concatenate([qc_ref[:, h * 128:(h + 1) * 128] for h in heads], axis=0)
        p = _softmax_cols(_dot_nt(kc2_ref[g], qc_g), cvalid)
        ocT = _dot(vcT_ref[g * HEAD_DIM:(g + 1) * HEAD_DIM, :], p.astype(BF16))
        ps = p[:, 0:tq] + p[:, tq:2 * tq] + p[:, 2 * tq:3 * tq] + p[:, 3 * tq:4 * tq]
        hi = ps.astype(BF16)
        lo = (ps - hi.astype(F32)).astype(BF16)
        impT = _dot(ovlT_ref[...], hi) + _dot(ovlT_ref[...], lo)
        forced = (jrow == 0) | (jrow == cur) | (jrow == cur - 1)
        e_ref[...] = jnp.where(jrow > cur, -1.0, jnp.where(forced, FORCED, impT))
        _rank_blocks_T(e_ref, rank_ref, jmax)
        selT = jnp.where(jrow <= cur, jnp.where(rank_ref[...] < float(SEL_TOPK), 0.0, NEG), NEG)
        self_g = selT.T.astype(BF16)
        lhs = jnp.concatenate(
            [jnp.concatenate([qr_ref[:, h * 128:(h + 1) * 128], self_g], axis=1) for h in heads], axis=0)

        def sel_body(kt, carry):
            m, l, acc = carry
            k0 = pl.multiple_of(kt * tk, tk)
            st = _dot_nt(kaug_ref[pl.ds(k0, tk), g * 256:(g + 1) * 256], lhs)
            kpos = k0 + lax.broadcasted_iota(jnp.int32, (tk, ncol), 0)
            st = jnp.where(kpos <= tcol, st, NEG)
            m_new = jnp.maximum(m, jnp.max(st, axis=0, keepdims=True))
            alpha = jnp.exp(m - m_new)
            pt = jnp.exp(st - m_new)
            l = alpha * l + jnp.sum(pt, axis=0, keepdims=True)
            acc = alpha * acc + _dot(vslcT_ref[kt, g * HEAD_DIM:(g + 1) * HEAD_DIM, :], pt.astype(BF16))
            return m_new, l, acc

        m, l, acc = lax.fori_loop(
            0, nkt, sel_body,
            (jnp.full((1, ncol), NEG, F32), jnp.zeros((1, ncol), F32), jnp.zeros((HEAD_DIM, ncol), F32)))
        osT = acc * (1.0 / l)
        qr_g = jnp.concatenate([qr_ref[:, h * 128:(h + 1) * 128] for h in heads], axis=0)
        sw = _dot_nt(kwin_ref[pl.ds(wstart, nwk), g * 128:(g + 1) * 128], qr_g)
        dist = tcol - (wstart + lax.broadcasted_iota(jnp.int32, (nwk, ncol), 0))
        pw = _softmax_cols(sw, jnp.where(dist >= 0, dist, WINDOW) < WINDOW).astype(BF16)
        owT = jnp.zeros((HEAD_DIM, ncol), F32)
        for i in range(nwk // 128):
            owT = owT + _dot(vwinT_ref[wstart // 128 + i, g * HEAD_DIM:(g + 1) * HEAD_DIM, :],
                             pw[i * 128:(i + 1) * 128, :])
        for mpair in range(NSA_REP // 2):
            parts = []
            for r in (2 * mpair, 2 * mpair + 1):
                h = heads[r]
                c = slice(r * tq, (r + 1) * tq)
                parts.append(gT[3 * h:3 * h + 1, :] * ocT[:, c] + gT[3 * h + 1:3 * h + 2, :] * osT[:, c]
                             + gT[3 * h + 2:3 * h + 3, :] * owT[:, c])
            ch = g * (NSA_REP // 2) + mpair
            o_ref[:, ch * 128:(ch + 1) * 128] = jnp.concatenate(parts, axis=0).T


def _nsa_prompt(qc, qr, gates, kc2, vcT, ovlT, kaug, vslcT, kwin, vwinT, *, b, s):
    tq, tk = Q_TILE, K_TILE
    nqb = s // tq
    nch = s // CMP_STRIDE
    assert s >= WINDOW + tq and s // SEL_BLOCK <= 128 and s % tk == 0
    qspec = lambda w: pl.BlockSpec((tq, w), lambda i, j: (i * nqb + j, 0))
    return pl.pallas_call(
        functools.partial(_nsa_prompt_kernel, tq=tq, tk=tk, nch=nch),
        grid=(b, nqb),
        in_specs=[
            qspec(1024), qspec(1024), qspec(128),
            pl.BlockSpec((None, 2, nch, 128), lambda i, j: (i, 0, 0, 0)),
            pl.BlockSpec((None, 128, nch), lambda i, j: (i, 0, 0)),
            pl.BlockSpec(ovlT.shape, lambda i, j: (0, 0)),
            pl.BlockSpec((s, 512), lambda i, j: (i, 0)),
            pl.BlockSpec((None, s // tk, 128, tk), lambda i, j: (i, 0, 0, 0)),
            pl.BlockSpec((s, 256), lambda i, j: (i, 0)),
            pl.BlockSpec((None, s // 128, 128, 128), lambda i, j: (i, 0, 0, 0)),
        ],
        out_specs=qspec(512),
        out_shape=jax.ShapeDtypeStruct((b * s, 512), F32),
        scratch_shapes=[pltpu.VMEM((128, 128), F32), pltpu.VMEM((128, 128), F32)],
        compiler_params=_cparams(("parallel", "arbitrary"), 52 << 20),
        name="nsa_prompt",
    )(qc, qr, gates, kc2, vcT, ovlT, kaug, vslcT, kwin, vwinT)


def _lambda(lq1_ref, lk1_ref, lq2_ref, lk2_ref, lam_init):
    a = jnp.sum(lq1_ref[...] * lk1_ref[...], axis=-1, keepdims=True)
    b = jnp.sum(lq2_ref[...] * lk2_ref[...], axis=-1, keepdims=True)
    return jnp.exp(a) - jnp.exp(b) + lam_init


def _diff_prompt_kernel(dq_ref, dk_ref, dvT_ref, lq1_ref, lk1_ref, lq2_ref, lk2_ref, o_ref, *, tq, tk, lam_init):
    qb = pl.program_id(1)
    t0 = qb * tq
    ncol = 2 * tq
    col = lax.broadcasted_iota(jnp.int32, (1, ncol), 1)
    tcol = t0 + (col & (tq - 1))
    nkt = (t0 + tq - 1) // tk + 1
    lam = _lambda(lq1_ref, lk1_ref, lq2_ref, lk2_ref, lam_init)
    low = lax.broadcasted_iota(jnp.int32, (tq, V7X_LANES), 1) < DIFF_QK
    zero = jnp.zeros((tq, V7X_LANES), BF16)
    for h in range(DIFF_HEADS):
        dqh = dq_ref[:, h * 128:(h + 1) * 128]
        qbd = jnp.concatenate([jnp.where(low, dqh, zero), jnp.where(low, zero, dqh)], axis=0)

        def body(kt, carry):
            m, l, acc = carry
            k0 = pl.multiple_of(kt * tk, tk)
            st = _dot_nt(dk_ref[pl.ds(k0, tk), h * 128:(h + 1) * 128], qbd)
            kpos = k0 + lax.broadcasted_iota(jnp.int32, (tk, ncol), 0)
            st = jnp.where(kpos <= tcol, st, NEG)
            m_new = jnp.maximum(m, jnp.max(st, axis=0, keepdims=True))
            alpha = jnp.exp(m - m_new)
            pt = jnp.exp(st - m_new)
            l = alpha * l + jnp.sum(pt, axis=0, keepdims=True)
            acc = alpha * acc + _dot(dvT_ref[kt, h * DIFF_V:(h + 1) * DIFF_V, :], pt.astype(BF16))
            return m_new, l, acc

        m, l, acc = lax.fori_loop(
            0, nkt, body,
            (jnp.full((1, ncol), NEG, F32), jnp.zeros((1, ncol), F32), jnp.zeros((DIFF_V, ncol), F32)))
        o = acc * (1.0 / l)
        o_ref[:, h * DIFF_V:(h + 1) * DIFF_V] = (o[:, 0:tq] - lam * o[:, tq:2 * tq]).T


def _diff_prompt(dq, dk, dvT, lams, *, b, s, lam_init):
    tq, tk = Q_TILE, K_TILE
    nqb = s // tq
    lspec = pl.BlockSpec((1, DIFF_QK), lambda i, j: (0, 0))
    return pl.pallas_call(
        functools.partial(_diff_prompt_kernel, tq=tq, tk=tk, lam_init=lam_init),
        grid=(b, nqb),
        in_specs=[
            pl.BlockSpec((tq, 512), lambda i, j: (i * nqb + j, 0)),
            pl.BlockSpec((s, 512), lambda i, j: (i, 0)),
            pl.BlockSpec((None, s // tk, 512, tk), lambda i, j: (i, 0, 0, 0)),
            lspec, lspec, lspec, lspec,
        ],
        out_specs=pl.BlockSpec((tq, 512), lambda i, j: (i * nqb + j, 0)),
        out_shape=jax.ShapeDtypeStruct((b * s, 512), F32),
        compiler_params=_cparams(("parallel", "arbitrary"), 52 << 20),
        name="diff_prompt",
    )(dq, dk, dvT, *lams)


def _page_specs(pp, width, lane_block):
    return [pl.BlockSpec((None, 128, width), functools.partial(
        lambda b, c, pt, i: (pt[b, c * pp + i], 0, lane_block), i=i)) for i in range(pp)]


def _s1_kernel(pt_ref, *refs, pp, nchk, nj):
    pages_kv = (refs[:pp], refs[pp:2 * pp])
    qc_ref, wpair_ref, pos_ref, w2bd_ref, ovl_ref, oc_ref, self_ref, ab_ref = refs[2 * pp:]
    c = pl.program_id(1)
    mrows = pp * 8
    lane_m = lax.broadcasted_iota(jnp.int32, (mrows, V7X_LANES), 1)
    low_m = lane_m < HEAD_DIM

    @pl.when(c == 0)
    def _():
        ab_ref[:, 2 * nchk:2 * nchk + 8, :] = jnp.zeros((2, 8, 2 * CMP_HIDDEN), F32)

    for s, pages in enumerate(pages_kv):
        acc = jnp.zeros((2 * mrows, 2 * CMP_HIDDEN), F32)
        for p in range(8):
            xa = jnp.concatenate([pg[pl.ds(2 * p, 8, stride=CMP_STRIDE), :] for pg in pages], axis=0)
            xb = jnp.concatenate([pg[pl.ds(2 * p + 1, 8, stride=CMP_STRIDE), :] for pg in pages], axis=0)
            acc = acc + _dot(_pair_lhs(xa, xb, low_m), wpair_ref[s, p])
        r0 = pl.multiple_of(c * mrows, 8)
        ab_ref[s, pl.ds(r0, mrows), :] = acc[0:mrows]
        ab_ref[s, pl.ds(nchk + r0, mrows), :] = acc[mrows:2 * mrows]

    @pl.when(c == pl.num_programs(1) - 1)
    def _():
        kc, vc = _compress_finalize(ab_ref, pos_ref, wpair_ref, w2bd_ref, nchk)
        kcb = kc.astype(BF16)
        vcb = vc.astype(BF16)
        nrow = NSA_REP * DEC_ROWS
        lane = lax.broadcasted_iota(jnp.int32, (DEC_ROWS, V7X_LANES), 1)
        low = lane < HEAD_DIM
        past = nchk * CMP_STRIDE
        trow = past + (lax.broadcasted_iota(jnp.int32, (nrow, nchk), 0) & (DEC_ROWS - 1))
        cvalid = (lax.broadcasted_iota(jnp.int32, (nrow, nchk), 1) * CMP_STRIDE + (CMP_BLOCK - 1)) <= trow
        for g in range(NSA_KV):
            heads = [g * NSA_REP + r for r in range(NSA_REP)]
            q = jnp.concatenate([qc_ref[:, h * 128:(h + 1) * 128] for h in heads], axis=0).astype(F32)
            if g == 1:
                q = pltpu.roll(q, 64, 1)
            s_c = jnp.where(cvalid, _dot_nt(q.astype(BF16), kcb), NEG)
            m = jnp.max(s_c, axis=1, keepdims=True)
            e = jnp.where(cvalid, jnp.exp(s_c - m), 0.0)
            l = jnp.sum(e, axis=1, keepdims=True)
            p = e * (1.0 / jnp.where(l > 0.0, l, 1.0))
            o = _dot(p.astype(BF16), vcb)
            for mpair in range(NSA_REP // 2):
                ev = o[(2 * mpair) * DEC_ROWS:(2 * mpair + 1) * DEC_ROWS, :]
                od = o[(2 * mpair + 1) * DEC_ROWS:(2 * mpair + 2) * DEC_ROWS, :]
                pair = jnp.where(low, ev, pltpu.roll(od, 64, 1)) if g == 0 else jnp.where(low, pltpu.roll(ev, 64, 1), od)
                ch = g * (NSA_REP // 2) + mpair
                oc_ref[:, ch * 128:(ch + 1) * 128] = pair
            ps = p[0:DEC_ROWS] + p[DEC_ROWS:2 * DEC_ROWS] + p[2 * DEC_ROWS:3 * DEC_ROWS] + p[3 * DEC_ROWS:4 * DEC_ROWS]
            hi = ps.astype(BF16)
            lo = (ps - hi.astype(F32)).astype(BF16)
            imp = _dot(hi, ovl_ref[...]) + _dot(lo, ovl_ref[...])
            ev = jnp.where(lane >= nj, -1.0, jnp.where((lane == 0) | (lane == nj - 1), FORCED, imp))
            rank = jnp.zeros((DEC_ROWS, V7X_LANES), F32)
            for jp in range(nj):
                cb = jnp.broadcast_to(ev[:, jp:jp + 1], (DEC_ROWS, V7X_LANES))
                rank = rank + jnp.where(lane > jp, jnp.where(cb >= ev, 1.0, 0.0), jnp.where(cb > ev, 1.0, 0.0))
            self_ref[g] = jnp.where(lane < nj, jnp.where(rank < float(SEL_TOPK - 1), 0.0, NEG), NEG)


def _s1(page_table, cache_nsa3, qc_s, wpair, pospair, w2bd, ovl, *, nseq, npages):
    pp = min(16, npages)
    assert npages % pp == 0
    nchk = npages * 128 // CMP_STRIDE
    nj = npages * 128 // SEL_BLOCK
    assert nj <= 128
    cmap3 = lambda b, c, pt: (0, 0, 0)
    grid_spec = pltpu.PrefetchScalarGridSpec(
        num_scalar_prefetch=1,
        grid=(nseq, npages // pp),
        in_specs=_page_specs(pp, 128, 0) + _page_specs(pp, 128, 1) + [
            pl.BlockSpec((None, DEC_ROWS, 1024), lambda b, c, pt: (b, 0, 0)),
            pl.BlockSpec(wpair.shape, lambda b, c, pt: (0, 0, 0, 0)),
            pl.BlockSpec(pospair.shape, cmap3),
            pl.BlockSpec(w2bd.shape, cmap3),
            pl.BlockSpec(ovl.shape, lambda b, c, pt: (0, 0)),
        ],
        out_specs=[
            pl.BlockSpec((None, DEC_ROWS, 512), lambda b, c, pt: (b, 0, 0)),
            pl.BlockSpec((None, 2, DEC_ROWS, 128), lambda b, c, pt: (b, 0, 0, 0)),
        ],
        scratch_shapes=[pltpu.VMEM((2, 2 * nchk + 8, 2 * CMP_HIDDEN), F32)],
    )
    return pl.pallas_call(
        functools.partial(_s1_kernel, pp=pp, nchk=nchk, nj=nj),
        grid_spec=grid_spec,
        out_shape=[jax.ShapeDtypeStruct((nseq, DEC_ROWS, 512), F32),
                   jax.ShapeDtypeStruct((nseq, 2, DEC_ROWS, 128), F32)],
        compiler_params=_cparams(("parallel", "arbitrary"), 32 << 20),
        name="sample_compress_select",
    )(page_table, *([cache_nsa3] * (2 * pp)), qc_s, wpair, pospair, w2bd, ovl)


def _rows_both_groups(q_ref):
    blocks = []
    for g in range(NSA_KV):
        qg = jnp.concatenate([q_ref[:, h * 128:(h + 1) * 128] for h in range(g * NSA_REP, (g + 1) * NSA_REP)], axis=0)
        if g == 1:
            qg = pltpu.roll(qg.astype(F32), 64, 1).astype(BF16)
        blocks.append(qg)
    return jnp.concatenate(blocks, axis=0)


def _s2_kernel(pt_ref, *refs, pp, past):
    pages = refs[:pp]
    (qr_ref, self_ref, gate_ref, oc_ref, nsa_new_ref, win_new_ref, state_ref,
     o_ref, m_ref, l_ref, acc_ref) = refs[pp:]
    c = pl.program_id(1)
    nrow = NSA_KV * NSA_REP * DEC_ROWS
    qall = _rows_both_groups(qr_ref)
    qrow = lax.broadcasted_iota(jnp.int32, (nrow, V7X_LANES), 0) & (DEC_ROWS - 1)
    lane = lax.broadcasted_iota(jnp.int32, (nrow, V7X_LANES), 1)

    @pl.when(c == 0)
    def _():
        m_ref[...] = jnp.full((nrow, 1), NEG, F32)
        l_ref[...] = jnp.zeros((nrow, 1), F32)
        acc_ref[...] = jnp.zeros((nrow, V7X_LANES), F32)

    sel_rows = jnp.concatenate([self_ref[g] for g in range(NSA_KV) for _ in range(NSA_REP)], axis=0).astype(BF16)
    nkeys = pp * 128
    blk = (c * nkeys + lax.broadcasted_iota(jnp.int32, (V7X_LANES, nkeys), 1)) >> 6
    expand = jnp.where(lax.broadcasted_iota(jnp.int32, (V7X_LANES, nkeys), 0) == blk, 1.0, 0.0).astype(BF16)
    s = jnp.concatenate([_dot_nt(qall, pg[:, 0:128].astype(BF16)) for pg in pages], axis=1) + _dot(sel_rows, expand)
    m_new = jnp.maximum(m_ref[...], jnp.max(s, axis=1, keepdims=True))
    alpha = jnp.exp(m_ref[...] - m_new)
    p = jnp.exp(s - m_new)
    l_ref[...] = alpha * l_ref[...] + jnp.sum(p, axis=1, keepdims=True)
    pv = jnp.zeros((nrow, V7X_LANES), F32)
    for i, pg in enumerate(pages):
        pv = pv + _dot(p[:, i * 128:(i + 1) * 128].astype(BF16), pg[:, 128:256].astype(BF16))
    acc_ref[...] = alpha * acc_ref[...] + pv
    m_ref[...] = m_new

    @pl.when(c == pl.num_programs(1) - 1)
    def _():
        zpad = jnp.zeros((V7X_LANES - DEC_ROWS, V7X_LANES), BF16)
        newvalid = lane <= qrow
        kn = jnp.concatenate([nsa_new_ref[:, 256:384].astype(BF16), zpad], axis=0)
        vn = jnp.concatenate([nsa_new_ref[:, 384:512].astype(BF16), zpad], axis=0)
        sn = jnp.where(newvalid, _dot_nt(qall, kn), NEG)
        m2 = jnp.maximum(m_ref[...], jnp.max(sn, axis=1, keepdims=True))
        a2 = jnp.exp(m_ref[...] - m2)
        pn = jnp.exp(sn - m2)
        l2 = a2 * l_ref[...] + jnp.sum(pn, axis=1, keepdims=True)
        o_s = (a2 * acc_ref[...] + _dot(pn.astype(BF16), vn)) * (1.0 / l2)
        wb = state_ref.shape[0]
        sw = _dot_nt(qall, state_ref[:, 0:128].astype(BF16))
        qrow_w = lax.broadcasted_iota(jnp.int32, (nrow, wb), 0) & (DEC_ROWS - 1)
        dist = (past + qrow_w) - (past - wb + lax.broadcasted_iota(jnp.int32, (nrow, wb), 1))
        sw = jnp.where(dist < WINDOW, sw, NEG)
        kwn = jnp.concatenate([win_new_ref[:, 0:128].astype(BF16), zpad], axis=0)
        vwn = jnp.concatenate([win_new_ref[:, 128:256].astype(BF16), zpad], axis=0)
        swn = jnp.where(newvalid, _dot_nt(qall, kwn), NEG)
        mw = jnp.maximum(jnp.max(sw, axis=1, keepdims=True), jnp.max(swn, axis=1, keepdims=True))
        pw = jnp.exp(sw - mw)
        pwn = jnp.exp(swn - mw)
        lw = jnp.sum(pw, axis=1, keepdims=True) + jnp.sum(pwn, axis=1, keepdims=True)
        o_w = (_dot(pw.astype(BF16), state_ref[:, 128:256].astype(BF16)) + _dot(pwn.astype(BF16), vwn)) * (1.0 / lw)
        low = lane[0:DEC_ROWS] < HEAD_DIM
        gates = gate_ref[...]

        def pair_rows(x, g, mpair):
            base = g * NSA_REP * DEC_ROWS
            ev = x[base + (2 * mpair) * DEC_ROWS:base + (2 * mpair + 1) * DEC_ROWS, :]
            od = x[base + (2 * mpair + 1) * DEC_ROWS:base + (2 * mpair + 2) * DEC_ROWS, :]
            if g == 0:
                return jnp.where(low, ev, pltpu.roll(od, 64, 1))
            return jnp.where(low, pltpu.roll(ev, 64, 1), od)

        for g in range(NSA_KV):
            for mpair in range(NSA_REP // 2):
                ch = g * (NSA_REP // 2) + mpair
                he, ho = 2 * ch, 2 * ch + 1
                out = jnp.zeros((DEC_ROWS, V7X_LANES), F32)
                branches = (oc_ref[:, ch * 128:(ch + 1) * 128], pair_rows(o_s, g, mpair), pair_rows(o_w, g, mpair))
                for i, br in enumerate(branches):
                    gcol = jnp.where(low, jnp.broadcast_to(gates[:, 3 * he + i:3 * he + i + 1], (DEC_ROWS, V7X_LANES)),
                                     jnp.broadcast_to(gates[:, 3 * ho + i:3 * ho + i + 1], (DEC_ROWS, V7X_LANES)))
                    out = out + gcol * br
                o_ref[:, ch * 128:(ch + 1) * 128] = out


def _s2(page_table, cache_nsa3, qr_s, selfeat, gates_s, oc_s, nsa_new, win_new, state_win, *, nseq, npages):
    pp = min(16, npages)
    assert npages % pp == 0
    wb = state_win.shape[1]
    past = npages * 128
    assert wb == WINDOW and past >= wb
    nrow = NSA_KV * NSA_REP * DEC_ROWS
    per_seq = lambda *shape: pl.BlockSpec((None,) + shape, lambda b, c, pt: (b,) + (0,) * len(shape))
    grid_spec = pltpu.PrefetchScalarGridSpec(
        num_scalar_prefetch=1,
        grid=(nseq, npages // pp),
        in_specs=_page_specs(pp, 256, 1) + [
            per_seq(DEC_ROWS, 1024), per_seq(2, DEC_ROWS, 128), per_seq(DEC_ROWS, 128), per_seq(DEC_ROWS, 512),
            per_seq(DEC_ROWS, 512), per_seq(DEC_ROWS, 256), per_seq(wb, 256),
        ],
        out_specs=per_seq(DEC_ROWS, 512),
        scratch_shapes=[pltpu.VMEM((nrow, 1), F32), pltpu.VMEM((nrow, 1), F32), pltpu.VMEM((nrow, V7X_LANES), F32)],
    )
    return pl.pallas_call(
        functools.partial(_s2_kernel, pp=pp, past=past),
        grid_spec=grid_spec,
        out_shape=jax.ShapeDtypeStruct((nseq, DEC_ROWS, 512), F32),
        compiler_params=_cparams(("parallel", "arbitrary"), 32 << 20),
        name="sample_select_window",
    )(page_table, *([cache_nsa3] * pp), qr_s, selfeat, gates_s, oc_s, nsa_new, win_new, state_win)


def _sdiff_kernel(pt_ref, *refs, pp, lam_init):
    pages = refs[:pp]
    dq_ref, new_ref, lq1_ref, lk1_ref, lq2_ref, lk2_ref, o_ref, m_ref, l_ref, acc_ref = refs[pp:]
    c = pl.program_id(1)
    nrow = DIFF_HEADS * 2 * DEC_ROWS
    lane8 = lax.broadcasted_iota(jnp.int32, (DEC_ROWS, V7X_LANES), 1)
    zero8 = jnp.zeros((DEC_ROWS, V7X_LANES), BF16)
    blocks = []
    for h in range(DIFF_HEADS):
        dqh = dq_ref[:, h * 128:(h + 1) * 128]
        for mp in range(2):
            piece = jnp.where((lane8 < DIFF_QK) if mp == 0 else (lane8 >= DIFF_QK), dqh, zero8)
            blocks.append(jnp.concatenate([piece if hh == h else zero8 for hh in range(DIFF_HEADS)], axis=1))
    q = jnp.concatenate(blocks, axis=0)

    @pl.when(c == 0)
    def _():
        m_ref[...] = jnp.full((nrow, 1), NEG, F32)
        l_ref[...] = jnp.zeros((nrow, 1), F32)
        acc_ref[...] = jnp.zeros((nrow, 512), F32)

    s = jnp.concatenate([_dot_nt(q, pg[:, 0:512].astype(BF16)) for pg in pages], axis=1)
    m_new = jnp.maximum(m_ref[...], jnp.max(s, axis=1, keepdims=True))
    alpha = jnp.exp(m_ref[...] - m_new)
    p = jnp.exp(s - m_new)
    l_ref[...] = alpha * l_ref[...] + jnp.sum(p, axis=1, keepdims=True)
    pv = jnp.zeros((nrow, 512), F32)
    for i, pg in enumerate(pages):
        pv = pv + _dot(p[:, i * 128:(i + 1) * 128].astype(BF16), pg[:, 512:1024].astype(BF16))
    acc_ref[...] = alpha * acc_ref[...] + pv
    m_ref[...] = m_new

    @pl.when(c == pl.num_programs(1) - 1)
    def _():
        zpad = jnp.zeros((V7X_LANES - DEC_ROWS, 512), BF16)
        kn = jnp.concatenate([new_ref[:, 0:512].astype(BF16), zpad], axis=0)
        vn = jnp.concatenate([new_ref[:, 512:1024].astype(BF16), zpad], axis=0)
        qrow = lax.broadcasted_iota(jnp.int32, (nrow, V7X_LANES), 0) & (DEC_ROWS - 1)
        lane = lax.broadcasted_iota(jnp.int32, (nrow, V7X_LANES), 1)
        sn = jnp.where(lane <= qrow, _dot_nt(q, kn), NEG)
        m2 = jnp.maximum(m_ref[...], jnp.max(sn, axis=1, keepdims=True))
        a2 = jnp.exp(m_ref[...] - m2)
        pn = jnp.exp(sn - m2)
        l2 = a2 * l_ref[...] + jnp.sum(pn, axis=1, keepdims=True)
        o = (a2 * acc_ref[...] + _dot(pn.astype(BF16), vn)) * (1.0 / l2)
        lam = _lambda(lq1_ref, lk1_ref, lq2_ref, lk2_ref, lam_init)
        for h in range(DIFF_HEADS):
            r1 = (2 * h) * DEC_ROWS
            r2 = (2 * h + 1) * DEC_ROWS
            o_ref[:, h * 128:(h + 1) * 128] = (o[r1:r1 + DEC_ROWS, h * 128:(h + 1) * 128]
                                               - lam * o[r2:r2 + DEC_ROWS, h * 128:(h + 1) * 128])


def _sdiff(page_table, cache_diff3, dq_s, diff_new, lams, *, nseq, npages, lam_init):
    pp = min(8, npages)
    assert npages % pp == 0
    nrow = DIFF_HEADS * 2 * DEC_ROWS
    lspec = pl.BlockSpec((1, DIFF_QK), lambda b, c, pt: (0, 0))
    grid_spec = pltpu.PrefetchScalarGridSpec(
        num_scalar_prefetch=1,
        grid=(nseq, npages // pp),
        in_specs=_page_specs(pp, 1024, 0) + [
            pl.BlockSpec((None, DEC_ROWS, 512), lambda b, c, pt: (b, 0, 0)),
            pl.BlockSpec((None, DEC_ROWS, 1024), lambda b, c, pt: (b, 0, 0)),
            lspec, lspec, lspec, lspec,
        ],
        out_specs=pl.BlockSpec((None, DEC_ROWS, 512), lambda b, c, pt: (b, 0, 0)),
        scratch_shapes=[pltpu.VMEM((nrow, 1), F32), pltpu.VMEM((nrow, 1), F32), pltpu.VMEM((nrow, 512), F32)],
    )
    return pl.pallas_call(
        functools.partial(_sdiff_kernel, pp=pp, lam_init=lam_init),
        grid_spec=grid_spec,
        out_shape=jax.ShapeDtypeStruct((nseq, DEC_ROWS, 512), F32),
        compiler_params=_cparams(("parallel", "arbitrary"), 32 << 20),
        name="sample_diff",
    )(page_table, *([cache_diff3] * pp), dq_s, diff_new, *lams)


def _post_kernel(x_ref, on_ref, od_ref, p_ref, dg_ref, wo_ref, gpost_ref, gfpre_ref, wup_ref, wdn_ref,
                 gfpost_ref, wple_ref, wgate_ref, gple_ref, y_ref, *, lam_init, ff_chunk):
    x = x_ref[...]
    dg = dg_ref[...] * (1.0 - lam_init)
    parts = [on_ref[...].astype(BF16)]
    for h in range(DIFF_HEADS):
        od = od_ref[:, h * DIFF_V:(h + 1) * DIFF_V]
        od = od * lax.rsqrt(jnp.mean(od * od, axis=-1, keepdims=True) + EPS) * dg
        parts.append(od.astype(BF16))
    cat = jnp.concatenate(parts, axis=1)
    x1 = x + _rms(_dot(cat, wo_ref[...]), gpost_ref[...])
    h1 = _rms(x1, gfpre_ref[...]).astype(BF16)
    f = jnp.zeros(x.shape, F32)
    for c in range(wup_ref.shape[1] // ff_chunk):
        u = jnp.maximum(_dot(h1, wup_ref[:, c * ff_chunk:(c + 1) * ff_chunk]), 0.0)
        f = f + _dot((u * u).astype(BF16), wdn_ref[c * ff_chunk:(c + 1) * ff_chunk, :])
    x2 = x1 + _rms(f, gfpost_ref[...])
    e = _dot(p_ref[...].astype(BF16), wple_ref[...]) * jax.nn.sigmoid(_dot(x2.astype(BF16), wgate_ref[...]))
    y_ref[...] = x2 + _rms(e, gple_ref[...])


def _post(x2d, o_n, o_d, p2d, dgain, wo, gpost, gfpre, wup, wdn, gfpost, wple, wgate, gple, *, lam_init):
    t, d = x2d.shape
    tm = min(TOK_TILE, t)
    assert t % tm == 0
    row = lambda w: pl.BlockSpec((tm, w), lambda i: (i, 0))
    full = lambda a: pl.BlockSpec(a.shape, lambda i: (0,) * a.ndim)
    consts = (dgain, wo, gpost, gfpre, wup, wdn, gfpost, wple, wgate, gple)
    return pl.pallas_call(
        functools.partial(_post_kernel, lam_init=lam_init, ff_chunk=1024),
        grid=(t // tm,),
        in_specs=[row(d), row(512), row(512), row(p2d.shape[1])] + [full(a) for a in consts],
        out_specs=row(d),
        out_shape=jax.ShapeDtypeStruct((t, d), F32),
        compiler_params=_cparams(("parallel",), 54 << 20),
        name="merge_mlp_ple",
    )(x2d, o_n, o_d, p2d, *consts)


def _rope_tables(pos):
    half = HEAD_DIM // 2
    inv = ROPE_THETA ** (-jnp.arange(half, dtype=F32) / half)
    ang = pos.astype(F32)[:, None] * inv[None, :]
    c, s = jnp.cos(ang), jnp.sin(ang)
    return jnp.concatenate([c, c, c, c], axis=-1), jnp.concatenate([-s, s, -s, s], axis=-1)


def _overlap_T(nch):
    cs = np.arange(nch)[None, :] * CMP_STRIDE
    bs = np.arange(128)[:, None] * SEL_BLOCK
    ov = np.clip(np.minimum(cs + CMP_BLOCK, bs + SEL_BLOCK) - np.maximum(cs, bs), 0, None) / CMP_BLOCK
    return ov.astype(np.float32)


def _pair_weights(w1):
    w1r = w1.reshape(CMP_BLOCK, HEAD_DIM, CMP_HIDDEN)
    a = w1r[:CMP_STRIDE].reshape(8, 128, CMP_HIDDEN)
    b = w1r[CMP_STRIDE:].reshape(8, 128, CMP_HIDDEN)
    return jnp.concatenate([a, b], axis=-1)


def _block_diag2(w2):
    z = jnp.zeros_like(w2)
    return jnp.concatenate([jnp.concatenate([w2, z], axis=1), jnp.concatenate([z, w2], axis=1)], axis=0)


def kernel(x_prompt, x_sample, cache_nsa, cache_diff, state_nsa_win, page_table, p_prompt, p_sample, g_mix_pre, w_in, cmp_pos, cmp_k_w1, cmp_k_w2, cmp_v_w1, cmp_v_w2, diff_lq1, diff_lk1, diff_lq2, diff_lk2, diff_gain, w_out, g_mix_post, g_ffn_pre, w_up, w_down, g_ffn_post, w_ple, w_ple_gate, g_ple):
    b, s, d = x_prompt.shape
    nseq, dec, _ = x_sample.shape
    depth, npool, page = cache_nsa.shape[0], cache_nsa.shape[1], cache_nsa.shape[2]
    npages = page_table.shape[1]
    past = npages * page
    wb = state_nsa_win.shape[2]
    assert page == 128 and dec <= DEC_ROWS and past % SEL_BLOCK == 0 and dec <= SEL_BLOCK
    assert (past + dec - CMP_BLOCK) // CMP_STRIDE + 1 == past // CMP_STRIDE - 1
    assert s % TOK_TILE == 0 and wb == WINDOW and s >= WINDOW

    cos_p, sin_p = _rope_tables(jnp.arange(s))
    pos_s = past + (jnp.arange(nseq * DEC_ROWS) % DEC_ROWS)
    cos_s, sin_s = _rope_tables(pos_s)
    ovlT = jnp.asarray(_overlap_T(s // CMP_STRIDE), BF16)
    ovl_s = jnp.asarray(_overlap_T(past // CMP_STRIDE).T, BF16)
    pt = page_table.astype(jnp.int32)

    xp = x_prompt.reshape(b * s, d)
    xs = jnp.pad(x_sample, ((0, 0), (0, DEC_ROWS - dec), (0, 0))).reshape(nseq * DEC_ROWS, d)
    outs = [[] for _ in range(6)]
    for i in range(depth):
        lam_init = 0.8 - 0.6 * math.exp(-0.3 * i)
        w = w_in[i]
        wcat = jnp.concatenate([w[:, :1280], w[:, 1304:], w[:, 1280:1304],
                                jnp.zeros((d, _C_END - _C_GATE - 24), w.dtype)], axis=1).astype(BF16)
        wpair = jnp.stack([_pair_weights(cmp_k_w1[i]), _pair_weights(cmp_v_w1[i])]).astype(BF16)
        pospair = jnp.stack([cmp_pos[i][:CMP_STRIDE].reshape(8, 128), cmp_pos[i][CMP_STRIDE:].reshape(8, 128)])
        w2bd = jnp.stack([_block_diag2(cmp_k_w2[i]), _block_diag2(cmp_v_w2[i])]).astype(BF16)
        lams = tuple(a[i].reshape(1, DIFF_QK) for a in (diff_lq1, diff_lk1, diff_lq2, diff_lk2))
        gpre = g_mix_pre[i].reshape(1, d)
        post_w = (diff_gain[i].reshape(1, DIFF_V), w_out[i].astype(BF16), g_mix_post[i].reshape(1, d),
                  g_ffn_pre[i].reshape(1, d), w_up[i].astype(BF16), w_down[i].astype(BF16),
                  g_ffn_post[i].reshape(1, d), w_ple[i].astype(BF16), w_ple_gate[i].astype(BF16),
                  g_ple[i].reshape(1, d))

        (qc, qr, dq, gates, nsa_rows, win_rows, diff_rows,
         kaug, vslcT, kwin, vwinT, dk, dvT) = _project(xp, gpre, wcat, cos_p, sin_p, seq_len=s, with_kv=True)
        kc2, vcT = _compress_prompt(nsa_rows, wpair, pospair, w2bd, b=b, s=s)
        o_n = _nsa_prompt(qc, qr, gates, kc2, vcT, ovlT, kaug, vslcT, kwin, vwinT, b=b, s=s)
        o_d = _diff_prompt(dq, dk, dvT, lams, b=b, s=s, lam_init=lam_init)
        xp = _post(xp, o_n, o_d, p_prompt[i].reshape(b * s, -1), *post_w, lam_init=lam_init)
        outs[0].append(nsa_rows.reshape(b, s, 4, NSA_KV, HEAD_DIM))
        outs[2].append(diff_rows.reshape(b, s, 2, DIFF_HEADS, DIFF_V))
        outs[4].append(win_rows.reshape(b, s, 2, NSA_KV, HEAD_DIM)[:, s - min(WINDOW, s):])

        qc_s, qr_s, dq_s, gates_s, nsa_new, win_new, diff_new = _project(
            xs, gpre, wcat, cos_s, sin_s, seq_len=None, with_kv=False)
        r3 = lambda a: a.reshape(nseq, DEC_ROWS, a.shape[-1])
        cache_nsa3 = cache_nsa[i].reshape(npool, page, 4 * NSA_KV * HEAD_DIM)
        cache_diff3 = cache_diff[i].reshape(npool, page, 2 * DIFF_HEADS * DIFF_V)
        oc_s, selfeat = _s1(pt, cache_nsa3, r3(qc_s), wpair, pospair, w2bd, ovl_s, nseq=nseq, npages=npages)
        state2 = state_nsa_win[i].reshape(nseq, wb, 2 * NSA_KV * HEAD_DIM)
        on_s = _s2(pt, cache_nsa3, r3(qr_s), selfeat, r3(gates_s), oc_s, r3(nsa_new), r3(win_new), state2,
                   nseq=nseq, npages=npages)
        od_s = _sdiff(pt, cache_diff3, r3(dq_s), r3(diff_new), lams, nseq=nseq, npages=npages, lam_init=lam_init)
        p_s = jnp.pad(p_sample[i], ((0, 0), (0, DEC_ROWS - dec), (0, 0))).reshape(nseq * DEC_ROWS, -1)
        xs = _post(xs, on_s.reshape(nseq * DEC_ROWS, 512), od_s.reshape(nseq * DEC_ROWS, 512), p_s, *post_w,
                   lam_init=lam_init)
        nsa_new4 = r3(nsa_new)[:, :dec]
        win_new4 = r3(win_new)[:, :dec]
        outs[1].append(nsa_new4.reshape(nseq, dec, 4, NSA_KV, HEAD_DIM))
        outs[3].append(r3(diff_new)[:, :dec].reshape(nseq, dec, 2, DIFF_HEADS, DIFF_V))
        win_all = jnp.concatenate([state2, win_new4], axis=1)
        keep = min(WINDOW, past + dec)
        outs[5].append(win_all[:, win_all.shape[1] - keep:].reshape(nseq, keep, 2, NSA_KV, HEAD_DIM))

    y_p = xp.reshape(b, s, d)
    y_s = xs.reshape(nseq, DEC_ROWS, d)[:, :dec]
    return (y_p, y_s) + tuple(jnp.stack(o) for o in outs)
```

```python
import functools
import math

import numpy as np
import jax
import jax.numpy as jnp
from jax import lax
from jax.experimental import pallas as pl
from jax.experimental.pallas import tpu as pltpu

F32 = jnp.float32
BF16 = jnp.bfloat16

HEAD_DIM = 64
NSA_HEADS = 8
NSA_KV = 2
NSA_REP = NSA_HEADS // NSA_KV
CMP_BLOCK = 32
CMP_STRIDE = 16
CMP_HIDDEN = 128
SEL_BLOCK = 64
SEL_TOPK = 16
WINDOW = 512
DIFF_HEADS = 4
DIFF_QK = 64
DIFF_V = 2 * DIFF_QK
ROPE_THETA = 10000.0
EPS = 1e-6
NEG = -1e30
FORCED = 1e9
Q_SCALE = (HEAD_DIM ** -0.5) * math.log2(math.e)

V7X_LANES = 128
V7X_SUBLANES = 8
V7X_BF16_SUBLANES = 16
V7X_VMEM_BYTES = 64 * 1024 * 1024
V7X_VMEM_USABLE = V7X_VMEM_BYTES - 8 * 1024 * 1024

DEC_ROWS = V7X_SUBLANES
ONES_ROWS = V7X_BF16_SUBLANES
Q_TILE = 128
DQ_TILE = 128
K_TILE = 512
TOK_TILE = 512

_C_Q = 0
_C_KV = 512
_C_DQ = 1280
_C_DK = 1792
_C_DV = 2304
_C_GATE = 2816
_C_END = 2944


def _cparams(sem, vmem_bytes):
    return pltpu.CompilerParams(dimension_semantics=sem,
                                vmem_limit_bytes=int(min(max(vmem_bytes, 16 << 20), V7X_VMEM_USABLE)))


def _dot(a, b):
    return jnp.dot(a, b, preferred_element_type=F32)


def _dot_nt(a, b):
    return lax.dot_general(a, b, (((1,), (1,)), ((), ())), preferred_element_type=F32)


def _rms(x, g):
    return x * lax.rsqrt(jnp.mean(x * x, axis=-1, keepdims=True) + EPS) * g


def _proj_kernel(x_ref, g_ref, w_ref, cos_ref, sin_ref, *outs, tm, nblk_tab, prompt):
    qc_ref, qr_ref, dq_ref, gate_ref, nsa_ref, win_ref, diff_ref = outs[:7]
    x = x_ref[...]
    h = _rms(x, g_ref[...]).astype(BF16)
    cos = cos_ref[...]
    sin = sin_ref[...]
    lane = lax.broadcasted_iota(jnp.int32, (tm, V7X_LANES), 1)
    low = lane < HEAD_DIM
    first = (lane & (HEAD_DIM - 1)) < (HEAD_DIM // 2)

    def rope(zc):
        partner = jnp.where(first, pltpu.roll(zc, 96, 1), pltpu.roll(zc, 32, 1))
        return zc * cos + partner * sin

    def pad_heads(zc):
        return jnp.where(low, zc, 0.0), jnp.where(low, pltpu.roll(zc, 64, 1), 0.0)

    def mm(a, b):
        return _dot(h, w_ref[:, a:b])

    zq = mm(_C_Q, _C_KV)
    for c in range(4):
        zc = zq[:, c * 128:(c + 1) * 128]
        e, o = pad_heads(zc * Q_SCALE)
        qc_ref[:, (2 * c) * 128:(2 * c + 1) * 128] = e.astype(BF16)
        qc_ref[:, (2 * c + 1) * 128:(2 * c + 2) * 128] = o.astype(BF16)
        e, o = pad_heads(rope(zc) * Q_SCALE)
        qr_ref[:, (2 * c) * 128:(2 * c + 1) * 128] = e.astype(BF16)
        qr_ref[:, (2 * c + 1) * 128:(2 * c + 2) * 128] = o.astype(BF16)

    zkv = mm(_C_KV, _C_DQ)
    k2 = rope(zkv[:, 256:384])
    v3 = zkv[:, 384:512]
    k4 = rope(zkv[:, 512:640])
    v5 = zkv[:, 640:768]
    win_ref[:, 0:128] = k4
    win_ref[:, 128:256] = v5

    zdq = mm(_C_DQ, _C_DK)
    for c in range(4):
        dq_ref[:, c * 128:(c + 1) * 128] = (rope(zdq[:, c * 128:(c + 1) * 128]) * Q_SCALE).astype(BF16)
    zdk = mm(_C_DK, _C_DV)
    dk_rot = [rope(zdk[:, c * 128:(c + 1) * 128]) for c in range(4)]
    zdv = mm(_C_DV, _C_GATE)
    gate_ref[...] = jax.nn.sigmoid(mm(_C_GATE, _C_END))

    if not prompt:
        nsa_ref[:, 0:256] = zkv[:, 0:256]
        nsa_ref[:, 256:384] = k2
        nsa_ref[:, 384:512] = v3
        for c in range(4):
            diff_ref[:, c * 128:(c + 1) * 128] = dk_rot[c]
        diff_ref[:, 512:1024] = zdv
        return

    cmp_ref, kaug_ref, vslcT_ref, kwin_ref, vwinT_ref, dk_ref, dvT_ref = outs[7:]
    cmp_ref[...] = zkv[:, 0:256]
    v3t = v3.T
    nsa_ref[0:128, :] = zkv[:, 0:128].T
    nsa_ref[128:256, :] = zkv[:, 128:256].T
    nsa_ref[256:384, :] = k2.T
    nsa_ref[384:512, :] = v3t
    for c in range(4):
        diff_ref[pl.ds(c, tm, stride=8), :] = dk_rot[c]
        diff_ref[pl.ds(4 + c, tm, stride=8), :] = zdv[:, c * 128:(c + 1) * 128]

    base = (pl.program_id(0) % nblk_tab) * tm
    pos = base + lax.broadcasted_iota(jnp.int32, (tm, V7X_LANES), 0)
    onehot = jnp.where(lane == (pos >> 6), 1.0, 0.0).astype(BF16)
    ones = jnp.ones((ONES_ROWS, tm), BF16)
    e, o = pad_heads(k2)
    kaug_ref[:, 0:128] = e.astype(BF16)
    kaug_ref[:, 128:256] = onehot
    kaug_ref[:, 256:384] = o.astype(BF16)
    kaug_ref[:, 384:512] = onehot
    for g in range(NSA_KV):
        vslcT_ref[g, 0:HEAD_DIM, :] = v3t[g * HEAD_DIM:(g + 1) * HEAD_DIM, :].astype(BF16)
        vslcT_ref[g, HEAD_DIM:HEAD_DIM + ONES_ROWS, :] = ones
    e, o = pad_heads(k4)
    kwin_ref[:, 0:128] = e.astype(BF16)
    kwin_ref[:, 128:256] = o.astype(BF16)
    v5t = v5.T.astype(BF16)
    for j in range(tm // 128):
        vwinT_ref[j] = v5t[:, j * 128:(j + 1) * 128]
    for c in range(4):
        dk_ref[:, c * 128:(c + 1) * 128] = dk_rot[c].astype(BF16)
        dvT_ref[c, 0:DIFF_V, :] = zdv[:, c * 128:(c + 1) * 128].T.astype(BF16)
        dvT_ref[c, DIFF_V:DIFF_V + ONES_ROWS, :] = ones


def _project(x2d, g, wcat, cos_tab, sin_tab, *, seq_len, prompt):
    t, d = x2d.shape
    tm = min(TOK_TILE, t)
    assert t % tm == 0 and cos_tab.shape[0] % tm == 0
    nblk_tab = cos_tab.shape[0] // tm
    nblk = t // tm
    row = lambda w: pl.BlockSpec((tm, w), lambda i: (i, 0))
    out_shape = [
        jax.ShapeDtypeStruct((t, 1024), BF16),
        jax.ShapeDtypeStruct((t, 1024), BF16),
        jax.ShapeDtypeStruct((t, 512), BF16),
        jax.ShapeDtypeStruct((t, 128), F32),
    ]
    out_specs = [row(1024), row(1024), row(512), row(128)]
    if not prompt:
        out_shape += [jax.ShapeDtypeStruct((t, 512), F32), jax.ShapeDtypeStruct((t, 256), F32),
                      jax.ShapeDtypeStruct((t, 1024), F32)]
        out_specs += [row(512), row(256), row(1024)]
    else:
        assert seq_len % tm == 0 and tm == K_TILE
        b = t // seq_len
        nt = seq_len // tm
        tile4 = lambda *blk: pl.BlockSpec((None, None) + blk, lambda i: (i // nt, i % nt, 0, 0, 0))
        out_shape += [
            jax.ShapeDtypeStruct((b, 512, seq_len), F32),
            jax.ShapeDtypeStruct((t, 256), F32),
            jax.ShapeDtypeStruct((t * 8, 128), F32),
            jax.ShapeDtypeStruct((t, 256), F32),
            jax.ShapeDtypeStruct((t, 512), BF16),
            jax.ShapeDtypeStruct((b, nt, NSA_KV, HEAD_DIM + ONES_ROWS, tm), BF16),
            jax.ShapeDtypeStruct((t, 256), BF16),
            jax.ShapeDtypeStruct((b, seq_len // 128, 128, 128), BF16),
            jax.ShapeDtypeStruct((t, 512), BF16),
            jax.ShapeDtypeStruct((b, nt, DIFF_HEADS, DIFF_V + ONES_ROWS, tm), BF16),
        ]
        out_specs += [
            pl.BlockSpec((None, 512, tm), lambda i: (i // nt, 0, i % nt)),
            row(256),
            pl.BlockSpec((tm * 8, 128), lambda i: (i, 0)),
            row(256),
            row(512),
            tile4(NSA_KV, HEAD_DIM + ONES_ROWS, tm),
            row(256),
            pl.BlockSpec((None, tm // 128, 128, 128), lambda i: (i // nt, i % nt, 0, 0)),
            row(512),
            tile4(DIFF_HEADS, DIFF_V + ONES_ROWS, tm),
        ]
    return pl.pallas_call(
        functools.partial(_proj_kernel, tm=tm, nblk_tab=nblk_tab, prompt=prompt),
        grid=(nblk,),
        in_specs=[
            row(d),
            pl.BlockSpec((1, d), lambda i: (0, 0)),
            pl.BlockSpec(wcat.shape, lambda i: (0, 0)),
            pl.BlockSpec((tm, 128), lambda i: (i % nblk_tab, 0)),
            pl.BlockSpec((tm, 128), lambda i: (i % nblk_tab, 0)),
        ],
        out_specs=out_specs,
        out_shape=out_shape,
        compiler_params=_cparams(("parallel",), 52 << 20),
        name="proj_prompt" if prompt else "proj_sample",
    )(x2d, g, wcat, cos_tab, sin_tab)


def _pair_lhs(xa, xb, low):
    l0 = jnp.where(low, xa, pltpu.roll(xb, 64, 1))
    l1 = jnp.where(low, pltpu.roll(xa, 64, 1), xb)
    return jnp.concatenate([l0, l1], axis=0).astype(BF16)


def _compress_bias(pos_ref, wpair_ref, s):
    r = jnp.zeros((16, 2 * CMP_HIDDEN), F32)
    for p in range(8):
        lhs = jnp.concatenate([jnp.broadcast_to(pos_ref[0, p:p + 1, :], (8, 128)),
                               jnp.broadcast_to(pos_ref[1, p:p + 1, :], (8, 128))], axis=0).astype(BF16)
        r = r + _dot(lhs, wpair_ref[s, p])
    return r[0:1, 0:CMP_HIDDEN] + r[8:9, CMP_HIDDEN:]


def _compress_finalize(ab_ref, pos_ref, wpair_ref, w2bd_ref, nch):
    row = lax.broadcasted_iota(jnp.int32, (nch, CMP_HIDDEN), 0)
    outs = []
    for s in range(2):
        bias = _compress_bias(pos_ref, wpair_ref, s)
        hid = []
        for g in range(NSA_KV):
            a = ab_ref[s, g * nch:(g + 1) * nch, 0:CMP_HIDDEN]
            bsh = ab_ref[s, pl.ds(g * nch + 1, nch), CMP_HIDDEN:2 * CMP_HIDDEN]
            hg = jax.nn.gelu(a + bsh + bias)
            hid.append(jnp.where(row < nch - 1, hg, 0.0))
        outs.append(_dot(jnp.concatenate(hid, axis=1).astype(BF16), w2bd_ref[s]))
    return outs[0], outs[1]


def _compress_prompt_kernel(rows_k_ref, rows_v_ref, wpair_ref, pos_ref, w2bd_ref, kc2_ref, vcT_ref, ab_ref, *, nch):
    lane = lax.broadcasted_iota(jnp.int32, (nch, V7X_LANES), 1)
    low = lane < HEAD_DIM
    ab_ref[:, 2 * nch:2 * nch + 8, :] = jnp.zeros((2, 8, 2 * CMP_HIDDEN), F32)
    for s, rows_ref in enumerate((rows_k_ref, rows_v_ref)):
        acc = jnp.zeros((2 * nch, 2 * CMP_HIDDEN), F32)
        for p in range(8):
            xa = rows_ref[pl.ds(2 * p, nch, stride=CMP_STRIDE), :]
            xb = rows_ref[pl.ds(2 * p + 1, nch, stride=CMP_STRIDE), :]
            acc = acc + _dot(_pair_lhs(xa, xb, low), wpair_ref[s, p])
        ab_ref[s, 0:2 * nch, :] = acc
    kc, vc = _compress_finalize(ab_ref, pos_ref, wpair_ref, w2bd_ref, nch)
    kc2_ref[0] = kc.astype(BF16)
    kc2_ref[1] = pltpu.roll(kc, 64, 1).astype(BF16)
    vcT_ref[...] = vc.T.astype(BF16)


def _compress_prompt(cmp_rows, wpair, pospair, w2bd, *, b, s):
    nch = s // CMP_STRIDE
    return pl.pallas_call(
        functools.partial(_compress_prompt_kernel, nch=nch),
        grid=(b,),
        in_specs=[
            pl.BlockSpec((s, 128), lambda i: (i, 0)),
            pl.BlockSpec((s, 128), lambda i: (i, 1)),
            pl.BlockSpec(wpair.shape, lambda i: (0, 0, 0, 0)),
            pl.BlockSpec(pospair.shape, lambda i: (0, 0, 0)),
            pl.BlockSpec(w2bd.shape, lambda i: (0, 0, 0)),
        ],
        out_specs=[
            pl.BlockSpec((None, 2, nch, 128), lambda i: (i, 0, 0, 0)),
            pl.BlockSpec((None, 128, nch), lambda i: (i, 0, 0)),
        ],
        out_shape=[jax.ShapeDtypeStruct((b, 2, nch, 128), BF16),
                   jax.ShapeDtypeStruct((b, 128, nch), BF16)],
        scratch_shapes=[pltpu.VMEM((2, 2 * nch + 8, 2 * CMP_HIDDEN), F32)],
        compiler_params=_cparams(("parallel",), 40 << 20),
        name="compress_prompt",
    )(cmp_rows, cmp_rows, wpair, pospair, w2bd)


def _flash_pipeline(nkt, nchain, qk_fn, v_fn, valid_fn, s_refs, p_ref, al_ref, m_ref, acc_ref, sub=128):
    tk = p_ref.shape[1]
    m_ref[...] = jnp.full(m_ref.shape, NEG, F32)
    acc_ref[...] = jnp.zeros(acc_ref.shape, F32)
    p_ref[...] = jnp.zeros(p_ref.shape, BF16)
    al_ref[...] = jnp.ones(al_ref.shape, F32)
    for c in range(nchain):
        s_refs[0][c] = qk_fn(c, 0)

    def step(i, cur, nxt, masked, do_qk):
        iprev = jnp.maximum(i - 1, 0)
        for c in range(nchain):
            acc_ref[c] = al_ref[c] * acc_ref[c] + _dot(v_fn(c, iprev), p_ref[c])
        if do_qk:
            for c in range(nchain):
                nxt[c] = qk_fn(c, i + 1)
        for c in range(nchain):
            m_old = m_ref[c]
            m_new = m_old
            for r in range(0, tk, sub):
                st = cur[c, r:r + sub, :]
                if masked:
                    st = jnp.where(valid_fn(i, r, sub), st, NEG)
                    cur[c, r:r + sub, :] = st
                m_new = jnp.maximum(m_new, jnp.max(st, axis=0, keepdims=True))
            al_ref[c] = jnp.exp2(m_old - m_new)
            m_ref[c] = m_new
            for r in range(0, tk, sub):
                p_ref[c, r:r + sub, :] = jnp.exp2(cur[c, r:r + sub, :] - m_new).astype(BF16)

    s0, s1 = s_refs
    npair = (nkt - 1) // 2

    def body(j, carry):
        step(2 * j, s0, s1, False, True)
        step(2 * j + 1, s1, s0, False, True)
        return carry

    lax.fori_loop(0, npair, body, 0)

    @pl.when(nkt - 1 - 2 * npair == 1)
    def _():
        step(nkt - 2, s0, s1, False, True)
        step(nkt - 1, s1, s0, True, False)

    @pl.when(nkt - 1 - 2 * npair == 0)
    def _():
        step(nkt - 1, s0, s1, True, False)

    for c in range(nchain):
        acc_ref[c] = al_ref[c] * acc_ref[c] + _dot(v_fn(c, nkt - 1), p_ref[c])


def _rank_blocks_T(e_ref, rank_ref, jmax):
    nv = 128 // 8
    srow = lax.broadcasted_iota(jnp.int32, (8, 128), 0)
    rank_ref[...] = jnp.zeros((128, 128), F32)
    for c in range(nv):
        @pl.when(8 * c <= jmax)
        def _():
            ec = e_ref[8 * c:8 * c + 8, :]
            rows = [jnp.broadcast_to(ec[r:r + 1, :], (8, 128)) for r in range(8)]
            for v in range(nv):
                ev = e_ref[8 * v:8 * v + 8, :]
                cnt = jnp.zeros((8, 128), F32)
                for r in range(8):
                    if v > c:
                        cnt = cnt + jnp.where(rows[r] >= ev, 1.0, 0.0)
                    elif v < c:
                        cnt = cnt + jnp.where(rows[r] > ev, 1.0, 0.0)
                    else:
                        cnt = cnt + jnp.where(srow > r, jnp.where(rows[r] >= ev, 1.0, 0.0),
                                              jnp.where(rows[r] > ev, 1.0, 0.0))
                rank_ref[8 * v:8 * v + 8, :] += cnt


def _nsa_prompt_kernel(qc_ref, qr_ref, gate_ref, kc2_ref, vcT_ref, ovlT_ref, kaug_ref, vslcT_ref,
                       kwin_ref, vwinT_ref, o_ref, e_ref, rank_ref, m_ref, acc_ref, ow_ref, s0_ref, s1_ref, p_ref, al_ref,
                       *, tq, tk, nch):
    qb = pl.program_id(1)
    t0 = qb * tq
    ncol = NSA_REP * tq
    hcol = ncol // 2
    col = lax.broadcasted_iota(jnp.int32, (1, ncol), 1)
    qcol = col & (tq - 1)
    tcol = t0 + qcol
    cidx = lax.broadcasted_iota(jnp.int32, (nch, ncol), 0)
    cvalid = (cidx * CMP_STRIDE + (CMP_BLOCK - 1)) <= tcol
    jrow = lax.broadcasted_iota(jnp.int32, (128, tq), 0)
    cur = (t0 + lax.broadcasted_iota(jnp.int32, (128, tq), 1)) >> 6
    jmax = (t0 + tq - 1) >> 6
    nkt = (t0 + tq - 1) // tk + 1
    nwin = WINDOW // tq
    gT = gate_ref[...].T

    ocT = []
    lhs = {}
    for g in range(NSA_KV):
        heads = [g * NSA_REP + r for r in range(NSA_REP)]
        qc_g = jnp.concatenate([qc_ref[:, h * 128:(h + 1) * 128] for h in heads], axis=0)
        sc = jnp.where(cvalid, _dot_nt(kc2_ref[g], qc_g), NEG)
        ec = jnp.exp2(sc - jnp.max(sc, axis=0, keepdims=True))
        lc = jnp.sum(ec, axis=0, keepdims=True)
        p = ec * jnp.where(tcol >= CMP_BLOCK - 1, 1.0 / lc, 0.0)
        ocT.append(_dot(vcT_ref[g * HEAD_DIM:(g + 1) * HEAD_DIM, :], p.astype(BF16)))
        ps = p[:, 0:tq] + p[:, tq:2 * tq] + p[:, 2 * tq:3 * tq] + p[:, 3 * tq:4 * tq]
        hi = ps.astype(BF16)
        lo = (ps - hi.astype(F32)).astype(BF16)
        impT = _dot(ovlT_ref[...], hi) + _dot(ovlT_ref[...], lo)
        forced = (jrow == 0) | (jrow == cur) | (jrow == cur - 1)
        e_ref[...] = jnp.where(jrow > cur, -1.0, jnp.where(forced, FORCED, impT))
        _rank_blocks_T(e_ref, rank_ref, jmax)
        selT = jnp.where(jrow <= cur, jnp.where(rank_ref[...] < float(SEL_TOPK), 0.0, NEG), NEG)
        self_g = selT.T.astype(BF16)
        lhs[g] = jnp.concatenate(
            [jnp.concatenate([qr_ref[:, h * 128:(h + 1) * 128], self_g], axis=1) for h in heads], axis=0)

    def sel_qk(g, kt):
        k0 = pl.multiple_of(kt * tk, tk)
        return _dot_nt(kaug_ref[pl.ds(k0, tk), g * 256:(g + 1) * 256], lhs[g])

    def sel_valid(kt, r, n):
        return (kt * tk + r + lax.broadcasted_iota(jnp.int32, (n, ncol), 0)) <= tcol

    _flash_pipeline(nkt, NSA_KV, sel_qk, lambda g, kt: vslcT_ref[kt, g], sel_valid,
                    (s0_ref, s1_ref), p_ref, al_ref, m_ref, acc_ref)

    @pl.when(qb >= nwin)
    def _():
        ii = lax.broadcasted_iota(jnp.int32, (tq, ncol), 0)
        for g in range(NSA_KV):
            qr_g = jnp.concatenate([qr_ref[:, h * 128:(h + 1) * 128]
                                    for h in range(g * NSA_REP, (g + 1) * NSA_REP)], axis=0)
            chunks = []
            for c in range(nwin + 1):
                kst = pl.multiple_of(t0 - WINDOW + c * tq, tq)
                sw = _dot_nt(kwin_ref[pl.ds(kst, tq), g * 128:(g + 1) * 128], qr_g)
                if c == 0:
                    sw = jnp.where(ii > qcol, sw, NEG)
                if c == nwin:
                    sw = jnp.where(ii <= qcol, sw, NEG)
                chunks.append(sw)
            m = jnp.max(functools.reduce(jnp.maximum, chunks), axis=0, keepdims=True)
            lw = jnp.zeros((1, ncol), F32)
            acc = jnp.zeros((HEAD_DIM, ncol), F32)
            for c in range(nwin + 1):
                ew = jnp.exp2(chunks[c] - m)
                lw = lw + jnp.sum(ew, axis=0, keepdims=True)
                acc = acc + _dot(vwinT_ref[qb - nwin + c, g * HEAD_DIM:(g + 1) * HEAD_DIM, :], ew.astype(BF16))
            ow_ref[g] = acc * (1.0 / lw)

    @pl.when(qb < nwin)
    def _():
        nwk = WINDOW + tq
        for g in range(NSA_KV):
            qr_g = jnp.concatenate([qr_ref[:, h * 128:(h + 1) * 128]
                                    for h in range(g * NSA_REP, (g + 1) * NSA_REP)], axis=0)
            sw = _dot_nt(kwin_ref[0:nwk, g * 128:(g + 1) * 128], qr_g)
            dist = tcol - lax.broadcasted_iota(jnp.int32, (nwk, ncol), 0)
            sw = jnp.where(dist >= 0, sw, NEG)
            ew = jnp.exp2(sw - jnp.max(sw, axis=0, keepdims=True))
            lw = jnp.sum(ew, axis=0, keepdims=True)
            ewb = ew.astype(BF16)
            acc = jnp.zeros((HEAD_DIM, ncol), F32)
            for i in range(nwk // 128):
                acc = acc + _dot(vwinT_ref[i, g * HEAD_DIM:(g + 1) * HEAD_DIM, :], ewb[i * 128:(i + 1) * 128, :])
            ow_ref[g] = acc * (1.0 / lw)

    for g in range(NSA_KV):
        owT = ow_ref[g]
        acc = acc_ref[g]
        osT_g = acc[0:HEAD_DIM] * (1.0 / acc[HEAD_DIM:HEAD_DIM + 1])
        for half in range(2):
            osT = osT_g[:, half * hcol:(half + 1) * hcol]
            parts = []
            for k, r in enumerate((2 * half, 2 * half + 1)):
                h = g * NSA_REP + r
                parts.append(gT[3 * h:3 * h + 1, :] * ocT[g][:, r * tq:(r + 1) * tq]
                             + gT[3 * h + 1:3 * h + 2, :] * osT[:, k * tq:(k + 1) * tq]
                             + gT[3 * h + 2:3 * h + 3, :] * owT[:, r * tq:(r + 1) * tq])
            ch = g * (NSA_REP // 2) + half
            o_ref[:, ch * 128:(ch + 1) * 128] = jnp.concatenate(parts, axis=0).T


def _nsa_prompt(qc, qr, gates, kc2, vcT, ovlT, kaug, vslcT, kwin, vwinT, *, b, s):
    tq, tk = Q_TILE, K_TILE
    nqb = s // tq
    nch = s // CMP_STRIDE
    assert s >= WINDOW + tq and s // SEL_BLOCK <= 128 and s % tk == 0 and tk % tq == 0
    qspec = lambda w: pl.BlockSpec((tq, w), lambda i, j: (i * nqb + j, 0))
    vrows = HEAD_DIM + ONES_ROWS
    ncol = NSA_REP * tq
    return pl.pallas_call(
        functools.partial(_nsa_prompt_kernel, tq=tq, tk=tk, nch=nch),
        grid=(b, nqb),
        in_specs=[
            qspec(1024), qspec(1024), qspec(128),
            pl.BlockSpec((None, 2, nch, 128), lambda i, j: (i, 0, 0, 0)),
            pl.BlockSpec((None, 128, nch), lambda i, j: (i, 0, 0)),
            pl.BlockSpec(ovlT.shape, lambda i, j: (0, 0)),
            pl.BlockSpec((s, 512), lambda i, j: (i, 0)),
            pl.BlockSpec((None, s // tk, NSA_KV, vrows, tk), lambda i, j: (i, 0, 0, 0, 0)),
            pl.BlockSpec((s, 256), lambda i, j: (i, 0)),
            pl.BlockSpec((None, s // 128, 128, 128), lambda i, j: (i, 0, 0, 0)),
        ],
        out_specs=qspec(512),
        out_shape=jax.ShapeDtypeStruct((b * s, 512), F32),
        scratch_shapes=[pltpu.VMEM((128, 128), F32), pltpu.VMEM((128, 128), F32),
                        pltpu.VMEM((NSA_KV, 1, ncol), F32),
                        pltpu.VMEM((NSA_KV, vrows, ncol), F32),
                        pltpu.VMEM((NSA_KV, HEAD_DIM, ncol), F32),
                        pltpu.VMEM((NSA_KV, tk, ncol), F32), pltpu.VMEM((NSA_KV, tk, ncol), F32),
                        pltpu.VMEM((NSA_KV, tk, ncol), BF16), pltpu.VMEM((NSA_KV, 1, ncol), F32)],
        compiler_params=_cparams(("parallel", "arbitrary"), 54 << 20),
        name="nsa_prompt",
    )(qc, qr, gates, kc2, vcT, ovlT, kaug, vslcT, kwin, vwinT)


def _lambda(lq1_ref, lk1_ref, lq2_ref, lk2_ref, lam_init):
    a = jnp.sum(lq1_ref[...] * lk1_ref[...], axis=-1, keepdims=True)
    b = jnp.sum(lq2_ref[...] * lk2_ref[...], axis=-1, keepdims=True)
    return jnp.exp(a) - jnp.exp(b) + lam_init


def _diff_prompt_kernel(dq_ref, dk_ref, dvT_ref, lq1_ref, lk1_ref, lq2_ref, lk2_ref, o_ref, m_ref, acc_ref,
                        s0_ref, s1_ref, p_ref, al_ref, *, tq, tk, lam_init):
    qb = pl.program_id(1)
    t0 = qb * tq
    ncol = 2 * tq
    tcol = t0 + (lax.broadcasted_iota(jnp.int32, (1, ncol), 1) & (tq - 1))
    nkt = (t0 + tq - 1) // tk + 1
    low = lax.broadcasted_iota(jnp.int32, (tq, V7X_LANES), 1) < DIFF_QK
    zero = jnp.zeros((tq, V7X_LANES), BF16)
    qbd = []
    for h in range(DIFF_HEADS):
        dqh = dq_ref[:, h * 128:(h + 1) * 128]
        qbd.append(jnp.concatenate([jnp.where(low, dqh, zero), jnp.where(low, zero, dqh)], axis=0))

    def qk(h, kt):
        k0 = pl.multiple_of(kt * tk, tk)
        return _dot_nt(dk_ref[pl.ds(k0, tk), h * 128:(h + 1) * 128], qbd[h])

    def valid(kt, r, n):
        return (kt * tk + r + lax.broadcasted_iota(jnp.int32, (n, ncol), 0)) <= tcol

    _flash_pipeline(nkt, DIFF_HEADS, qk, lambda h, kt: dvT_ref[kt, h], valid,
                    (s0_ref, s1_ref), p_ref, al_ref, m_ref, acc_ref)
    lam = _lambda(lq1_ref, lk1_ref, lq2_ref, lk2_ref, lam_init)
    for h in range(DIFF_HEADS):
        acc = acc_ref[h]
        o = acc[0:DIFF_V] * (1.0 / acc[DIFF_V:DIFF_V + 1])
        o_ref[:, h * DIFF_V:(h + 1) * DIFF_V] = (o[:, 0:tq] - lam * o[:, tq:2 * tq]).T


def _diff_prompt(dq, dk, dvT, lams, *, b, s, lam_init):
    tq, tk = DQ_TILE, K_TILE
    assert s % tq == 0 and tk % tq == 0
    nqb = s // tq
    vrows = DIFF_V + ONES_ROWS
    ncol = 2 * tq
    lspec = pl.BlockSpec((1, DIFF_QK), lambda i, j: (0, 0))
    return pl.pallas_call(
        functools.partial(_diff_prompt_kernel, tq=tq, tk=tk, lam_init=lam_init),
        grid=(b, nqb),
        in_specs=[
            pl.BlockSpec((tq, 512), lambda i, j: (i * nqb + j, 0)),
            pl.BlockSpec((s, 512), lambda i, j: (i, 0)),
            pl.BlockSpec((None, s // tk, DIFF_HEADS, vrows, tk), lambda i, j: (i, 0, 0, 0, 0)),
            lspec, lspec, lspec, lspec,
        ],
        out_specs=pl.BlockSpec((tq, 512), lambda i, j: (i * nqb + j, 0)),
        out_shape=jax.ShapeDtypeStruct((b * s, 512), F32),
        scratch_shapes=[pltpu.VMEM((DIFF_HEADS, 1, ncol), F32), pltpu.VMEM((DIFF_HEADS, vrows, ncol), F32),
                        pltpu.VMEM((DIFF_HEADS, tk, ncol), F32), pltpu.VMEM((DIFF_HEADS, tk, ncol), F32),
                        pltpu.VMEM((DIFF_HEADS, tk, ncol), BF16), pltpu.VMEM((DIFF_HEADS, 1, ncol), F32)],
        compiler_params=_cparams(("parallel", "arbitrary"), 54 << 20),
        name="diff_prompt",
    )(dq, dk, dvT, *lams)


def _page_specs(pp, rows, row_block):
    return [pl.BlockSpec((None, rows, 128), functools.partial(
        lambda b, c, pt, i: (pt[b, c * pp + i], row_block, 0), i=i)) for i in range(pp)]


def _s1_kernel(pt_ref, *refs, pp, nchk, nj):
    pages_kv = (refs[:pp], refs[pp:2 * pp])
    qc_ref, wpair_ref, pos_ref, w2bd_ref, ovl_ref, oc_ref, self_ref, ab_ref, slab_ref = refs[2 * pp:]
    c = pl.program_id(1)
    mrows = pp * 8
    lane_m = lax.broadcasted_iota(jnp.int32, (mrows, V7X_LANES), 1)
    low_m = lane_m < HEAD_DIM

    @pl.when(c == 0)
    def _():
        ab_ref[:, 2 * nchk:2 * nchk + 8, :] = jnp.zeros((2, 8, 2 * CMP_HIDDEN), F32)

    for s, pages in enumerate(pages_kv):
        for i, pg in enumerate(pages):
            slab_ref[s, i * 128:(i + 1) * 128, :] = pg[...].T
        acc = jnp.zeros((2 * mrows, 2 * CMP_HIDDEN), F32)
        for p in range(8):
            xa = slab_ref[s, pl.ds(2 * p, mrows, stride=CMP_STRIDE), :]
            xb = slab_ref[s, pl.ds(2 * p + 1, mrows, stride=CMP_STRIDE), :]
            acc = acc + _dot(_pair_lhs(xa, xb, low_m), wpair_ref[s, p])
        r0 = pl.multiple_of(c * mrows, 8)
        ab_ref[s, pl.ds(r0, mrows), :] = acc[0:mrows]
        ab_ref[s, pl.ds(nchk + r0, mrows), :] = acc[mrows:2 * mrows]

    @pl.when(c == pl.num_programs(1) - 1)
    def _():
        kc, vc = _compress_finalize(ab_ref, pos_ref, wpair_ref, w2bd_ref, nchk)
        kcb = kc.astype(BF16)
        vcb = vc.astype(BF16)
        nrow = NSA_REP * DEC_ROWS
        lane = lax.broadcasted_iota(jnp.int32, (DEC_ROWS, V7X_LANES), 1)
        low = lane < HEAD_DIM
        past = nchk * CMP_STRIDE
        trow = past + (lax.broadcasted_iota(jnp.int32, (nrow, nchk), 0) & (DEC_ROWS - 1))
        cvalid = (lax.broadcasted_iota(jnp.int32, (nrow, nchk), 1) * CMP_STRIDE + (CMP_BLOCK - 1)) <= trow
        for g in range(NSA_KV):
            heads = [g * NSA_REP + r for r in range(NSA_REP)]
            q = jnp.concatenate([qc_ref[:, h * 128:(h + 1) * 128] for h in heads], axis=0).astype(F32)
            if g == 1:
                q = pltpu.roll(q, 64, 1)
            s_c = jnp.where(cvalid, _dot_nt(q.astype(BF16), kcb), NEG)
            m = jnp.max(s_c, axis=1, keepdims=True)
            e = jnp.where(cvalid, jnp.exp2(s_c - m), 0.0)
            l = jnp.sum(e, axis=1, keepdims=True)
            p = e * (1.0 / jnp.where(l > 0.0, l, 1.0))
            o = _dot(p.astype(BF16), vcb)
            for mpair in range(NSA_REP // 2):
                ev = o[(2 * mpair) * DEC_ROWS:(2 * mpair + 1) * DEC_ROWS, :]
                od = o[(2 * mpair + 1) * DEC_ROWS:(2 * mpair + 2) * DEC_ROWS, :]
                pair = jnp.where(low, ev, pltpu.roll(od, 64, 1)) if g == 0 else jnp.where(low, pltpu.roll(ev, 64, 1), od)
                ch = g * (NSA_REP // 2) + mpair
                oc_ref[:, ch * 128:(ch + 1) * 128] = pair
            ps = p[0:DEC_ROWS] + p[DEC_ROWS:2 * DEC_ROWS] + p[2 * DEC_ROWS:3 * DEC_ROWS] + p[3 * DEC_ROWS:4 * DEC_ROWS]
            hi = ps.astype(BF16)
            lo = (ps - hi.astype(F32)).astype(BF16)
            imp = _dot(hi, ovl_ref[...]) + _dot(lo, ovl_ref[...])
            ev = jnp.where(lane >= nj, -1.0, jnp.where((lane == 0) | (lane == nj - 1), FORCED, imp))
            rank = jnp.zeros((DEC_ROWS, V7X_LANES), F32)
            for jp in range(nj):
                cb = jnp.broadcast_to(ev[:, jp:jp + 1], (DEC_ROWS, V7X_LANES))
                rank = rank + jnp.where(lane > jp, jnp.where(cb >= ev, 1.0, 0.0), jnp.where(cb > ev, 1.0, 0.0))
            self_ref[g] = jnp.where(lane < nj, jnp.where(rank < float(SEL_TOPK - 1), 0.0, NEG), NEG)


def _s1(page_table, cache_nsaT, qc_s, wpair, pospair, w2bd, ovl, *, nseq, npages):
    pp = min(16, npages)
    assert npages % pp == 0
    nchk = npages * 128 // CMP_STRIDE
    nj = npages * 128 // SEL_BLOCK
    assert nj <= 128
    cmap3 = lambda b, c, pt: (0, 0, 0)
    grid_spec = pltpu.PrefetchScalarGridSpec(
        num_scalar_prefetch=1,
        grid=(nseq, npages // pp),
        in_specs=_page_specs(pp, 128, 0) + _page_specs(pp, 128, 1) + [
            pl.BlockSpec((None, DEC_ROWS, 1024), lambda b, c, pt: (b, 0, 0)),
            pl.BlockSpec(wpair.shape, lambda b, c, pt: (0, 0, 0, 0)),
            pl.BlockSpec(pospair.shape, cmap3),
            pl.BlockSpec(w2bd.shape, cmap3),
            pl.BlockSpec(ovl.shape, lambda b, c, pt: (0, 0)),
        ],
        out_specs=[
            pl.BlockSpec((None, DEC_ROWS, 512), lambda b, c, pt: (b, 0, 0)),
            pl.BlockSpec((None, 2, DEC_ROWS, 128), lambda b, c, pt: (b, 0, 0, 0)),
        ],
        scratch_shapes=[pltpu.VMEM((2, 2 * nchk + 8, 2 * CMP_HIDDEN), F32), pltpu.VMEM((2, pp * 128, 128), F32)],
    )
    return pl.pallas_call(
        functools.partial(_s1_kernel, pp=pp, nchk=nchk, nj=nj),
        grid_spec=grid_spec,
        out_shape=[jax.ShapeDtypeStruct((nseq, DEC_ROWS, 512), F32),
                   jax.ShapeDtypeStruct((nseq, 2, DEC_ROWS, 128), F32)],
        compiler_params=_cparams(("parallel", "arbitrary"), 32 << 20),
        name="sample_compress_select",
    )(page_table, *([cache_nsaT] * (2 * pp)), qc_s, wpair, pospair, w2bd, ovl)


def _rows_both_groups(q_ref):
    blocks = []
    for g in range(NSA_KV):
        qg = jnp.concatenate([q_ref[:, h * 128:(h + 1) * 128] for h in range(g * NSA_REP, (g + 1) * NSA_REP)], axis=0)
        if g == 1:
            qg = pltpu.roll(qg.astype(F32), 64, 1).astype(BF16)
        blocks.append(qg)
    return jnp.concatenate(blocks, axis=0)


def _s2_kernel(pt_ref, *refs, pp, past):
    pages = refs[:pp]
    (qr_ref, self_ref, gate_ref, oc_ref, nsa_new_ref, win_new_ref, state_ref,
     o_ref, m_ref, l_ref, acc_ref) = refs[pp:]
    c = pl.program_id(1)
    nrow = NSA_KV * NSA_REP * DEC_ROWS
    qall = _rows_both_groups(qr_ref)
    qrow = lax.broadcasted_iota(jnp.int32, (nrow, V7X_LANES), 0) & (DEC_ROWS - 1)
    lane = lax.broadcasted_iota(jnp.int32, (nrow, V7X_LANES), 1)

    @pl.when(c == 0)
    def _():
        m_ref[...] = jnp.full((nrow, 1), NEG, F32)
        l_ref[...] = jnp.zeros((nrow, 1), F32)
        acc_ref[...] = jnp.zeros((nrow, V7X_LANES), F32)

    sel_rows = jnp.concatenate([self_ref[g] for g in range(NSA_KV) for _ in range(NSA_REP)], axis=0).astype(BF16)
    nkeys = pp * 128
    blk = (c * nkeys + lax.broadcasted_iota(jnp.int32, (V7X_LANES, nkeys), 1)) >> 6
    expand = jnp.where(lax.broadcasted_iota(jnp.int32, (V7X_LANES, nkeys), 0) == blk, 1.0, 0.0).astype(BF16)
    s = jnp.concatenate([_dot(qall, pg[0:128, :].astype(BF16)) for pg in pages], axis=1) + _dot(sel_rows, expand)
    m_new = jnp.maximum(m_ref[...], jnp.max(s, axis=1, keepdims=True))
    alpha = jnp.exp2(m_ref[...] - m_new)
    p = jnp.exp2(s - m_new)
    l_ref[...] = alpha * l_ref[...] + jnp.sum(p, axis=1, keepdims=True)
    pv = jnp.zeros((nrow, V7X_LANES), F32)
    for i, pg in enumerate(pages):
        pv = pv + _dot_nt(p[:, i * 128:(i + 1) * 128].astype(BF16), pg[128:256, :].astype(BF16))
    acc_ref[...] = alpha * acc_ref[...] + pv
    m_ref[...] = m_new

    @pl.when(c == pl.num_programs(1) - 1)
    def _():
        zpad = jnp.zeros((V7X_LANES - DEC_ROWS, V7X_LANES), BF16)
        newvalid = lane <= qrow
        kn = jnp.concatenate([nsa_new_ref[:, 256:384].astype(BF16), zpad], axis=0)
        vn = jnp.concatenate([nsa_new_ref[:, 384:512].astype(BF16), zpad], axis=0)
        sn = jnp.where(newvalid, _dot_nt(qall, kn), NEG)
        m2 = jnp.maximum(m_ref[...], jnp.max(sn, axis=1, keepdims=True))
        a2 = jnp.exp2(m_ref[...] - m2)
        pn = jnp.exp2(sn - m2)
        l2 = a2 * l_ref[...] + jnp.sum(pn, axis=1, keepdims=True)
        o_s = (a2 * acc_ref[...] + _dot(pn.astype(BF16), vn)) * (1.0 / l2)
        wb = state_ref.shape[1]
        sw = _dot(qall, state_ref[0:128, :].astype(BF16))
        qrow_w = lax.broadcasted_iota(jnp.int32, (nrow, wb), 0) & (DEC_ROWS - 1)
        dist = (past + qrow_w) - (past - wb + lax.broadcasted_iota(jnp.int32, (nrow, wb), 1))
        sw = jnp.where(dist < WINDOW, sw, NEG)
        kwn = jnp.concatenate([win_new_ref[:, 0:128].astype(BF16), zpad], axis=0)
        vwn = jnp.concatenate([win_new_ref[:, 128:256].astype(BF16), zpad], axis=0)
        swn = jnp.where(newvalid, _dot_nt(qall, kwn), NEG)
        mw = jnp.maximum(jnp.max(sw, axis=1, keepdims=True), jnp.max(swn, axis=1, keepdims=True))
        pw = jnp.exp2(sw - mw)
        pwn = jnp.exp2(swn - mw)
        lw = jnp.sum(pw, axis=1, keepdims=True) + jnp.sum(pwn, axis=1, keepdims=True)
        o_w = (_dot_nt(pw.astype(BF16), state_ref[128:256, :].astype(BF16)) + _dot(pwn.astype(BF16), vwn)) * (1.0 / lw)
        low = lane[0:DEC_ROWS] < HEAD_DIM
        gates = gate_ref[...]

        def pair_rows(x, g, mpair):
            base = g * NSA_REP * DEC_ROWS
            ev = x[base + (2 * mpair) * DEC_ROWS:base + (2 * mpair + 1) * DEC_ROWS, :]
            od = x[base + (2 * mpair + 1) * DEC_ROWS:base + (2 * mpair + 2) * DEC_ROWS, :]
            if g == 0:
                return jnp.where(low, ev, pltpu.roll(od, 64, 1))
            return jnp.where(low, pltpu.roll(ev, 64, 1), od)

        for g in range(NSA_KV):
            for mpair in range(NSA_REP // 2):
                ch = g * (NSA_REP // 2) + mpair
                he, ho = 2 * ch, 2 * ch + 1
                out = jnp.zeros((DEC_ROWS, V7X_LANES), F32)
                branches = (oc_ref[:, ch * 128:(ch + 1) * 128], pair_rows(o_s, g, mpair), pair_rows(o_w, g, mpair))
                for i, br in enumerate(branches):
                    gcol = jnp.where(low, jnp.broadcast_to(gates[:, 3 * he + i:3 * he + i + 1], (DEC_ROWS, V7X_LANES)),
                                     jnp.broadcast_to(gates[:, 3 * ho + i:3 * ho + i + 1], (DEC_ROWS, V7X_LANES)))
                    out = out + gcol * br
                o_ref[:, ch * 128:(ch + 1) * 128] = out


def _s2(page_table, cache_nsaT, qr_s, selfeat, gates_s, oc_s, nsa_new, win_new, state_t, *, nseq, npages):
    pp = min(16, npages)
    assert npages % pp == 0
    wb = state_t.shape[2]
    past = npages * 128
    assert wb == WINDOW and past >= wb
    nrow = NSA_KV * NSA_REP * DEC_ROWS
    per_seq = lambda *shape: pl.BlockSpec((None,) + shape, lambda b, c, pt: (b,) + (0,) * len(shape))
    grid_spec = pltpu.PrefetchScalarGridSpec(
        num_scalar_prefetch=1,
        grid=(nseq, npages // pp),
        in_specs=_page_specs(pp, 256, 1) + [
            per_seq(DEC_ROWS, 1024), per_seq(2, DEC_ROWS, 128), per_seq(DEC_ROWS, 128), per_seq(DEC_ROWS, 512),
            per_seq(DEC_ROWS, 512), per_seq(DEC_ROWS, 256), per_seq(256, wb),
        ],
        out_specs=per_seq(DEC_ROWS, 512),
        scratch_shapes=[pltpu.VMEM((nrow, 1), F32), pltpu.VMEM((nrow, 1), F32), pltpu.VMEM((nrow, V7X_LANES), F32)],
    )
    return pl.pallas_call(
        functools.partial(_s2_kernel, pp=pp, past=past),
        grid_spec=grid_spec,
        out_shape=jax.ShapeDtypeStruct((nseq, DEC_ROWS, 512), F32),
        compiler_params=_cparams(("parallel", "arbitrary"), 32 << 20),
        name="sample_select_window",
    )(page_table, *([cache_nsaT] * pp), qr_s, selfeat, gates_s, oc_s, nsa_new, win_new, state_t)


def _sdiff_kernel(pt_ref, *refs, pp, lam_init):
    pages = refs[:pp]
    dq_ref, new_ref, lq1_ref, lk1_ref, lq2_ref, lk2_ref, o_ref, m_ref, l_ref, acc_ref = refs[pp:]
    c = pl.program_id(1)
    hrow = 2 * DEC_ROWS
    nrow = DIFF_HEADS * hrow
    lane8 = lax.broadcasted_iota(jnp.int32, (DEC_ROWS, V7X_LANES), 1)
    zero8 = jnp.zeros((DEC_ROWS, V7X_LANES), BF16)
    qh = []
    for h in range(DIFF_HEADS):
        dqh = dq_ref[:, h * 128:(h + 1) * 128]
        qh.append(jnp.concatenate([jnp.where(lane8 < DIFF_QK, dqh, zero8), jnp.where(lane8 < DIFF_QK, zero8, dqh)], axis=0))

    @pl.when(c == 0)
    def _():
        m_ref[...] = jnp.full((nrow, 1), NEG, F32)
        l_ref[...] = jnp.zeros((nrow, 1), F32)
        acc_ref[...] = jnp.zeros((nrow, DIFF_V), F32)

    def keys(pg, h):
        return pg[pl.ds(h, 128, stride=2 * DIFF_HEADS), :].astype(BF16)

    def vals(pg, h):
        return pg[pl.ds(DIFF_HEADS + h, 128, stride=2 * DIFF_HEADS), :].astype(BF16)

    s = jnp.concatenate(
        [jnp.concatenate([_dot_nt(qh[h], keys(pg, h)) for pg in pages], axis=1) for h in range(DIFF_HEADS)], axis=0)
    m_new = jnp.maximum(m_ref[...], jnp.max(s, axis=1, keepdims=True))
    alpha = jnp.exp2(m_ref[...] - m_new)
    p = jnp.exp2(s - m_new)
    l_ref[...] = alpha * l_ref[...] + jnp.sum(p, axis=1, keepdims=True)
    pvs = []
    for h in range(DIFF_HEADS):
        pv = jnp.zeros((hrow, DIFF_V), F32)
        for i, pg in enumerate(pages):
            pv = pv + _dot(p[h * hrow:(h + 1) * hrow, i * 128:(i + 1) * 128].astype(BF16), vals(pg, h))
        pvs.append(pv)
    acc_ref[...] = alpha * acc_ref[...] + jnp.concatenate(pvs, axis=0)
    m_ref[...] = m_new

    @pl.when(c == pl.num_programs(1) - 1)
    def _():
        zpad = jnp.zeros((V7X_LANES - DEC_ROWS, V7X_LANES), BF16)
        qrow = lax.broadcasted_iota(jnp.int32, (hrow, V7X_LANES), 0) & (DEC_ROWS - 1)
        lane = lax.broadcasted_iota(jnp.int32, (hrow, V7X_LANES), 1)
        lam = _lambda(lq1_ref, lk1_ref, lq2_ref, lk2_ref, lam_init)
        for h in range(DIFF_HEADS):
            rows = slice(h * hrow, (h + 1) * hrow)
            kn = jnp.concatenate([new_ref[:, h * 128:(h + 1) * 128].astype(BF16), zpad], axis=0)
            vn = jnp.concatenate([new_ref[:, 512 + h * 128:512 + (h + 1) * 128].astype(BF16), zpad], axis=0)
            sn = jnp.where(lane <= qrow, _dot_nt(qh[h], kn), NEG)
            m1 = m_ref[rows, :]
            m2 = jnp.maximum(m1, jnp.max(sn, axis=1, keepdims=True))
            a2 = jnp.exp2(m1 - m2)
            pn = jnp.exp2(sn - m2)
            l2 = a2 * l_ref[rows, :] + jnp.sum(pn, axis=1, keepdims=True)
            o = (a2 * acc_ref[rows, :] + _dot(pn.astype(BF16), vn)) * (1.0 / l2)
            o_ref[:, h * 128:(h + 1) * 128] = o[0:DEC_ROWS] - lam * o[DEC_ROWS:hrow]


def _sdiff(page_table, cache_diff8, dq_s, diff_new, lams, *, nseq, npages, lam_init):
    pp = min(8, npages)
    assert npages % pp == 0
    nrow = DIFF_HEADS * 2 * DEC_ROWS
    lspec = pl.BlockSpec((1, DIFF_QK), lambda b, c, pt: (0, 0))
    grid_spec = pltpu.PrefetchScalarGridSpec(
        num_scalar_prefetch=1,
        grid=(nseq, npages // pp),
        in_specs=_page_specs(pp, 1024, 0) + [
            pl.BlockSpec((None, DEC_ROWS, 512), lambda b, c, pt: (b, 0, 0)),
            pl.BlockSpec((None, DEC_ROWS, 1024), lambda b, c, pt: (b, 0, 0)),
            lspec, lspec, lspec, lspec,
        ],
        out_specs=pl.BlockSpec((None, DEC_ROWS, 512), lambda b, c, pt: (b, 0, 0)),
        scratch_shapes=[pltpu.VMEM((nrow, 1), F32), pltpu.VMEM((nrow, 1), F32), pltpu.VMEM((nrow, DIFF_V), F32)],
    )
    return pl.pallas_call(
        functools.partial(_sdiff_kernel, pp=pp, lam_init=lam_init),
        grid_spec=grid_spec,
        out_shape=jax.ShapeDtypeStruct((nseq, DEC_ROWS, 512), F32),
        compiler_params=_cparams(("parallel", "arbitrary"), 32 << 20),
        name="sample_diff",
    )(page_table, *([cache_diff8] * pp), dq_s, diff_new, *lams)


def _post_kernel(x_ref, on_ref, od_ref, p_ref, dg_ref, wo_ref, gpost_ref, gfpre_ref, wup_ref, wdn_ref,
                 gfpost_ref, wple_ref, wgate_ref, gple_ref, y_ref, *, lam_init, ff_chunk):
    x = x_ref[...]
    dg = dg_ref[...] * (1.0 - lam_init)
    parts = [on_ref[...].astype(BF16)]
    for h in range(DIFF_HEADS):
        od = od_ref[:, h * DIFF_V:(h + 1) * DIFF_V]
        od = od * lax.rsqrt(jnp.mean(od * od, axis=-1, keepdims=True) + EPS) * dg
        parts.append(od.astype(BF16))
    cat = jnp.concatenate(parts, axis=1)
    x1 = x + _rms(_dot(cat, wo_ref[...]), gpost_ref[...])
    h1 = _rms(x1, gfpre_ref[...]).astype(BF16)
    f = jnp.zeros(x.shape, F32)
    for c in range(wup_ref.shape[1] // ff_chunk):
        u = jnp.maximum(_dot(h1, wup_ref[:, c * ff_chunk:(c + 1) * ff_chunk]), 0.0)
        f = f + _dot((u * u).astype(BF16), wdn_ref[c * ff_chunk:(c + 1) * ff_chunk, :])
    x2 = x1 + _rms(f, gfpost_ref[...])
    e = _dot(p_ref[...].astype(BF16), wple_ref[...]) * jax.nn.sigmoid(_dot(x2.astype(BF16), wgate_ref[...]))
    y_ref[...] = x2 + _rms(e, gple_ref[...])


def _post(x2d, o_n, o_d, p2d, dgain, wo, gpost, gfpre, wup, wdn, gfpost, wple, wgate, gple, *, lam_init):
    t, d = x2d.shape
    tm = min(TOK_TILE, t)
    assert t % tm == 0
    row = lambda w: pl.BlockSpec((tm, w), lambda i: (i, 0))
    full = lambda a: pl.BlockSpec(a.shape, lambda i: (0,) * a.ndim)
    consts = (dgain, wo, gpost, gfpre, wup, wdn, gfpost, wple, wgate, gple)
    return pl.pallas_call(
        functools.partial(_post_kernel, lam_init=lam_init, ff_chunk=1024),
        grid=(t // tm,),
        in_specs=[row(d), row(512), row(512), row(p2d.shape[1])] + [full(a) for a in consts],
        out_specs=row(d),
        out_shape=jax.ShapeDtypeStruct((t, d), F32),
        compiler_params=_cparams(("parallel",), 54 << 20),
        name="merge_mlp_ple",
    )(x2d, o_n, o_d, p2d, *consts)


def _rope_tables(pos):
    half = HEAD_DIM // 2
    inv = ROPE_THETA ** (-jnp.arange(half, dtype=F32) / half)
    ang = pos.astype(F32)[:, None] * inv[None, :]
    c, s = jnp.cos(ang), jnp.sin(ang)
    return jnp.concatenate([c, c, c, c], axis=-1), jnp.concatenate([-s, s, -s, s], axis=-1)


def _overlap_T(nch):
    cs = np.arange(nch)[None, :] * CMP_STRIDE
    bs = np.arange(128)[:, None] * SEL_BLOCK
    ov = np.clip(np.minimum(cs + CMP_BLOCK, bs + SEL_BLOCK) - np.maximum(cs, bs), 0, None) / CMP_BLOCK
    return ov.astype(np.float32)


def _pair_weights(w1):
    w1r = w1.reshape(CMP_BLOCK, HEAD_DIM, CMP_HIDDEN)
    a = w1r[:CMP_STRIDE].reshape(8, 128, CMP_HIDDEN)
    b = w1r[CMP_STRIDE:].reshape(8, 128, CMP_HIDDEN)
    return jnp.concatenate([a, b], axis=-1)


def _block_diag2(w2):
    z = jnp.zeros_like(w2)
    return jnp.concatenate([jnp.concatenate([w2, z], axis=1), jnp.concatenate([z, w2], axis=1)], axis=0)


def _token_last(a):
    nd = a.ndim
    t = jnp.transpose(a, (0, 1) + tuple(range(3, nd)) + (2,))
    return t.reshape(a.shape[0], a.shape[1], -1, a.shape[2])


def kernel(x_prompt, x_sample, cache_nsa, cache_diff, state_nsa_win, page_table, p_prompt, p_sample, g_mix_pre, w_in, cmp_pos, cmp_k_w1, cmp_k_w2, cmp_v_w1, cmp_v_w2, diff_lq1, diff_lk1, diff_lq2, diff_lk2, diff_gain, w_out, g_mix_post, g_ffn_pre, w_up, w_down, g_ffn_post, w_ple, w_ple_gate, g_ple):
    b, s, d = x_prompt.shape
    nseq, dec, _ = x_sample.shape
    depth, npool, page = cache_nsa.shape[0], cache_nsa.shape[1], cache_nsa.shape[2]
    npages = page_table.shape[1]
    past = npages * page
    wb = state_nsa_win.shape[2]
    assert page == 128 and dec <= DEC_ROWS and past % SEL_BLOCK == 0 and dec <= SEL_BLOCK
    assert (past + dec - CMP_BLOCK) // CMP_STRIDE + 1 == past // CMP_STRIDE - 1
    assert s % TOK_TILE == 0 and wb == WINDOW and s >= WINDOW

    cos_p, sin_p = _rope_tables(jnp.arange(s))
    pos_s = past + (jnp.arange(nseq * DEC_ROWS) % DEC_ROWS)
    cos_s, sin_s = _rope_tables(pos_s)
    ovlT = jnp.asarray(_overlap_T(s // CMP_STRIDE), BF16)
    ovl_s = jnp.asarray(_overlap_T(past // CMP_STRIDE).T, BF16)
    pt = page_table.astype(jnp.int32)
    cache_nsaT = _token_last(cache_nsa).reshape(depth * npool, 4 * NSA_KV * HEAD_DIM, page)
    cache_diff8 = cache_diff.reshape(depth * npool, page * 2 * DIFF_HEADS, DIFF_V)
    state_t = _token_last(state_nsa_win)

    xp = x_prompt.reshape(b * s, d)
    xs = jnp.pad(x_sample, ((0, 0), (0, DEC_ROWS - dec), (0, 0))).reshape(nseq * DEC_ROWS, d)
    outs = [[] for _ in range(6)]
    for i in range(depth):
        lam_init = 0.8 - 0.6 * math.exp(-0.3 * i)
        w = w_in[i]
        wcat = jnp.concatenate([w[:, :1280], w[:, 1304:], w[:, 1280:1304],
                                jnp.zeros((d, _C_END - _C_GATE - 24), w.dtype)], axis=1).astype(BF16)
        wpair = jnp.stack([_pair_weights(cmp_k_w1[i]), _pair_weights(cmp_v_w1[i])]).astype(BF16)
        pospair = jnp.stack([cmp_pos[i][:CMP_STRIDE].reshape(8, 128), cmp_pos[i][CMP_STRIDE:].reshape(8, 128)])
        w2bd = jnp.stack([_block_diag2(cmp_k_w2[i]), _block_diag2(cmp_v_w2[i])]).astype(BF16)
        lams = tuple(a[i].reshape(1, DIFF_QK) for a in (diff_lq1, diff_lk1, diff_lq2, diff_lk2))
        gpre = g_mix_pre[i].reshape(1, d)
        post_w = (diff_gain[i].reshape(1, DIFF_V), w_out[i].astype(BF16), g_mix_post[i].reshape(1, d),
                  g_ffn_pre[i].reshape(1, d), w_up[i].astype(BF16), w_down[i].astype(BF16),
                  g_ffn_post[i].reshape(1, d), w_ple[i].astype(BF16), w_ple_gate[i].astype(BF16),
                  g_ple[i].reshape(1, d))
        pt_i = pt + i * npool

        (qc, qr, dq, gates, nsa_t, win_rows, diff8, cmp_rows,
         kaug, vslcT, kwin, vwinT, dk, dvT) = _project(xp, gpre, wcat, cos_p, sin_p, seq_len=s, prompt=True)
        kc2, vcT = _compress_prompt(cmp_rows, wpair, pospair, w2bd, b=b, s=s)
        o_n = _nsa_prompt(qc, qr, gates, kc2, vcT, ovlT, kaug, vslcT, kwin, vwinT, b=b, s=s)
        o_d = _diff_prompt(dq, dk, dvT, lams, b=b, s=s, lam_init=lam_init)
        xp = _post(xp, o_n, o_d, p_prompt[i].reshape(b * s, -1), *post_w, lam_init=lam_init)
        outs[0].append(jnp.transpose(nsa_t.reshape(b, 4, NSA_KV, HEAD_DIM, s), (0, 4, 1, 2, 3)))
        outs[2].append(diff8.reshape(b, s, 2, DIFF_HEADS, DIFF_V))
        win_tail = lax.optimization_barrier(win_rows.reshape(b, s, 2 * NSA_KV * HEAD_DIM)[:, s - min(WINDOW, s):])
        outs[4].append(win_tail.reshape(b, min(WINDOW, s), 2, NSA_KV, HEAD_DIM))

        qc_s, qr_s, dq_s, gates_s, nsa_new, win_new, diff_new = _project(
            xs, gpre, wcat, cos_s, sin_s, seq_len=None, prompt=False)
        r3 = lambda a: a.reshape(nseq, DEC_ROWS, a.shape[-1])
        oc_s, selfeat = _s1(pt_i, cache_nsaT, r3(qc_s), wpair, pospair, w2bd, ovl_s, nseq=nseq, npages=npages)
        on_s = _s2(pt_i, cache_nsaT, r3(qr_s), selfeat, r3(gates_s), oc_s, r3(nsa_new), r3(win_new), state_t[i],
                   nseq=nseq, npages=npages)
        od_s = _sdiff(pt_i, cache_diff8, r3(dq_s), r3(diff_new), lams, nseq=nseq, npages=npages, lam_init=lam_init)
        p_s = jnp.pad(p_sample[i], ((0, 0), (0, DEC_ROWS - dec), (0, 0))).reshape(nseq * DEC_ROWS, -1)
        xs = _post(xs, on_s.reshape(nseq * DEC_ROWS, 512), od_s.reshape(nseq * DEC_ROWS, 512), p_s, *post_w,
                   lam_init=lam_init)
        win_new4 = r3(win_new)[:, :dec].reshape(nseq, dec, 2, NSA_KV, HEAD_DIM)
        outs[1].append(r3(nsa_new)[:, :dec].reshape(nseq, dec, 4, NSA_KV, HEAD_DIM))
        outs[3].append(r3(diff_new)[:, :dec].reshape(nseq, dec, 2, DIFF_HEADS, DIFF_V))
        win_all = jnp.concatenate([state_nsa_win[i], win_new4], axis=1)
        keep = min(WINDOW, past + dec)
        outs[5].append(win_all[:, win_all.shape[1] - keep:])

    y_p = xp.reshape(b, s, d)
    y_s = xs.reshape(nseq, DEC_ROWS, d)[:, :dec]
    return (y_p, y_s) + tuple(jnp.stack(o) for o in outs)
```

```python
import functools
import math

import numpy as np
import jax
import jax.numpy as jnp
from jax import lax
from jax.experimental import pallas as pl
from jax.experimental.pallas import tpu as pltpu

F32 = jnp.float32
BF16 = jnp.bfloat16

HEAD_DIM = 64
NSA_HEADS = 8
NSA_KV = 2
NSA_REP = NSA_HEADS // NSA_KV
CMP_BLOCK = 32
CMP_STRIDE = 16
CMP_HIDDEN = 128
SEL_BLOCK = 64
SEL_TOPK = 16
WINDOW = 512
DIFF_HEADS = 4
DIFF_QK = 64
DIFF_V = 2 * DIFF_QK
ROPE_THETA = 10000.0
EPS = 1e-6
NEG = -1e30
FORCED = 1e9
Q_SCALE = (HEAD_DIM ** -0.5) * math.log2(math.e)

V7X_LANES = 128
V7X_SUBLANES = 8
V7X_BF16_SUBLANES = 16
V7X_VMEM_BYTES = 64 * 1024 * 1024
V7X_VMEM_USABLE = V7X_VMEM_BYTES - 8 * 1024 * 1024

DEC_ROWS = V7X_SUBLANES
ONES_ROWS = V7X_BF16_SUBLANES
Q_TILE = 128
DQ_TILE = 128
K_TILE = 512
TOK_TILE = 512

_C_Q = 0
_C_KV = 512
_C_DQ = 1280
_C_DK = 1792
_C_DV = 2304
_C_GATE = 2816
_C_END = 2944


def _cparams(sem, vmem_bytes):
    return pltpu.CompilerParams(dimension_semantics=sem,
                                vmem_limit_bytes=int(min(max(vmem_bytes, 16 << 20), V7X_VMEM_USABLE)))


def _dot(a, b):
    return jnp.dot(a, b, preferred_element_type=F32)


def _dot_nt(a, b):
    return lax.dot_general(a, b, (((1,), (1,)), ((), ())), preferred_element_type=F32)


def _rms(x, g):
    return x * lax.rsqrt(jnp.mean(x * x, axis=-1, keepdims=True) + EPS) * g


def _proj_kernel(x_ref, g_ref, w_ref, cos_ref, sin_ref, *outs, tm, nblk_tab, prompt):
    qc_ref, qr_ref, dq_ref, gate_ref, nsa_ref, win_ref, diff_ref = outs[:7]
    x = x_ref[...]
    h = _rms(x, g_ref[...]).astype(BF16)
    cos = cos_ref[...]
    sin = sin_ref[...]
    lane = lax.broadcasted_iota(jnp.int32, (tm, V7X_LANES), 1)
    low = lane < HEAD_DIM
    first = (lane & (HEAD_DIM - 1)) < (HEAD_DIM // 2)

    def rope(zc):
        partner = jnp.where(first, pltpu.roll(zc, 96, 1), pltpu.roll(zc, 32, 1))
        return zc * cos + partner * sin

    def pad_heads(zc):
        return jnp.where(low, zc, 0.0), jnp.where(low, pltpu.roll(zc, 64, 1), 0.0)

    def mm(a, b):
        return _dot(h, w_ref[:, a:b])

    zq = mm(_C_Q, _C_KV)
    for c in range(4):
        zc = zq[:, c * 128:(c + 1) * 128]
        e, o = pad_heads(zc * Q_SCALE)
        qc_ref[:, (2 * c) * 128:(2 * c + 1) * 128] = e.astype(BF16)
        qc_ref[:, (2 * c + 1) * 128:(2 * c + 2) * 128] = o.astype(BF16)
        e, o = pad_heads(rope(zc) * Q_SCALE)
        qr_ref[:, (2 * c) * 128:(2 * c + 1) * 128] = e.astype(BF16)
        qr_ref[:, (2 * c + 1) * 128:(2 * c + 2) * 128] = o.astype(BF16)

    zkv = mm(_C_KV, _C_DQ)
    k2 = rope(zkv[:, 256:384])
    v3 = zkv[:, 384:512]
    k4 = rope(zkv[:, 512:640])
    v5 = zkv[:, 640:768]
    win_ref[:, 0:128] = k4
    win_ref[:, 128:256] = v5

    zdq = mm(_C_DQ, _C_DK)
    for c in range(4):
        dq_ref[:, c * 128:(c + 1) * 128] = (rope(zdq[:, c * 128:(c + 1) * 128]) * Q_SCALE).astype(BF16)
    zdk = mm(_C_DK, _C_DV)
    dk_rot = [rope(zdk[:, c * 128:(c + 1) * 128]) for c in range(4)]
    zdv = mm(_C_DV, _C_GATE)
    gate_ref[...] = jax.nn.sigmoid(mm(_C_GATE, _C_END))

    if not prompt:
        nsa_ref[:, 0:256] = zkv[:, 0:256]
        nsa_ref[:, 256:384] = k2
        nsa_ref[:, 384:512] = v3
        for c in range(4):
            diff_ref[:, c * 128:(c + 1) * 128] = dk_rot[c]
        diff_ref[:, 512:1024] = zdv
        return

    cmp_ref, kaug_ref, vslcT_ref, kwin_ref, vwinT_ref, dk_ref, dvT_ref = outs[7:]
    cmp_ref[...] = zkv[:, 0:256]
    v3t = v3.T
    nsa_ref[0:128, :] = zkv[:, 0:128].T
    nsa_ref[128:256, :] = zkv[:, 128:256].T
    nsa_ref[256:384, :] = k2.T
    nsa_ref[384:512, :] = v3t
    for c in range(4):
        diff_ref[pl.ds(c, tm, stride=8), :] = dk_rot[c]
        diff_ref[pl.ds(4 + c, tm, stride=8), :] = zdv[:, c * 128:(c + 1) * 128]

    base = (pl.program_id(0) % nblk_tab) * tm
    pos = base + lax.broadcasted_iota(jnp.int32, (tm, V7X_LANES), 0)
    onehot = jnp.where(lane == (pos >> 6), 1.0, 0.0).astype(BF16)
    ones = jnp.ones((ONES_ROWS, tm), BF16)
    e, o = pad_heads(k2)
    kaug_ref[:, 0:128] = e.astype(BF16)
    kaug_ref[:, 128:256] = onehot
    kaug_ref[:, 256:384] = o.astype(BF16)
    kaug_ref[:, 384:512] = onehot
    for g in range(NSA_KV):
        vslcT_ref[g, 0:HEAD_DIM, :] = v3t[g * HEAD_DIM:(g + 1) * HEAD_DIM, :].astype(BF16)
        vslcT_ref[g, HEAD_DIM:HEAD_DIM + ONES_ROWS, :] = ones
    e, o = pad_heads(k4)
    kwin_ref[:, 0:128] = e.astype(BF16)
    kwin_ref[:, 128:256] = o.astype(BF16)
    v5t = v5.T.astype(BF16)
    for j in range(tm // 128):
        vwinT_ref[j] = v5t[:, j * 128:(j + 1) * 128]
    for c in range(4):
        dk_ref[:, c * 128:(c + 1) * 128] = dk_rot[c].astype(BF16)
        dvT_ref[c, 0:DIFF_V, :] = zdv[:, c * 128:(c + 1) * 128].T.astype(BF16)
        dvT_ref[c, DIFF_V:DIFF_V + ONES_ROWS, :] = ones


def _project(x2d, g, wcat, cos_tab, sin_tab, *, seq_len, prompt):
    t, d = x2d.shape
    tm = min(TOK_TILE, t)
    assert t % tm == 0 and cos_tab.shape[0] % tm == 0
    nblk_tab = cos_tab.shape[0] // tm
    nblk = t // tm
    row = lambda w: pl.BlockSpec((tm, w), lambda i: (i, 0))
    out_shape = [
        jax.ShapeDtypeStruct((t, 1024), BF16),
        jax.ShapeDtypeStruct((t, 1024), BF16),
        jax.ShapeDtypeStruct((t, 512), BF16),
        jax.ShapeDtypeStruct((t, 128), F32),
    ]
    out_specs = [row(1024), row(1024), row(512), row(128)]
    if not prompt:
        out_shape += [jax.ShapeDtypeStruct((t, 512), F32), jax.ShapeDtypeStruct((t, 256), F32),
                      jax.ShapeDtypeStruct((t, 1024), F32)]
        out_specs += [row(512), row(256), row(1024)]
    else:
        assert seq_len % tm == 0 and tm == K_TILE
        b = t // seq_len
        nt = seq_len // tm
        tile4 = lambda *blk: pl.BlockSpec((None, None) + blk, lambda i: (i // nt, i % nt, 0, 0, 0))
        out_shape += [
            jax.ShapeDtypeStruct((b, 512, seq_len), F32),
            jax.ShapeDtypeStruct((t, 256), F32),
            jax.ShapeDtypeStruct((t * 8, 128), F32),
            jax.ShapeDtypeStruct((t, 256), F32),
            jax.ShapeDtypeStruct((t, 512), BF16),
            jax.ShapeDtypeStruct((b, nt, NSA_KV, HEAD_DIM + ONES_ROWS, tm), BF16),
            jax.ShapeDtypeStruct((t, 256), BF16),
            jax.ShapeDtypeStruct((b, seq_len // 128, 128, 128), BF16),
            jax.ShapeDtypeStruct((t, 512), BF16),
            jax.ShapeDtypeStruct((b, nt, DIFF_HEADS, DIFF_V + ONES_ROWS, tm), BF16),
        ]
        out_specs += [
            pl.BlockSpec((None, 512, tm), lambda i: (i // nt, 0, i % nt)),
            row(256),
            pl.BlockSpec((tm * 8, 128), lambda i: (i, 0)),
            row(256),
            row(512),
            tile4(NSA_KV, HEAD_DIM + ONES_ROWS, tm),
            row(256),
            pl.BlockSpec((None, tm // 128, 128, 128), lambda i: (i // nt, i % nt, 0, 0)),
            row(512),
            tile4(DIFF_HEADS, DIFF_V + ONES_ROWS, tm),
        ]
    return pl.pallas_call(
        functools.partial(_proj_kernel, tm=tm, nblk_tab=nblk_tab, prompt=prompt),
        grid=(nblk,),
        in_specs=[
            row(d),
            pl.BlockSpec((1, d), lambda i: (0, 0)),
            pl.BlockSpec(wcat.shape, lambda i: (0, 0)),
            pl.BlockSpec((tm, 128), lambda i: (i % nblk_tab, 0)),
            pl.BlockSpec((tm, 128), lambda i: (i % nblk_tab, 0)),
        ],
        out_specs=out_specs,
        out_shape=out_shape,
        compiler_params=_cparams(("parallel",), 52 << 20),
        name="proj_prompt" if prompt else "proj_sample",
    )(x2d, g, wcat, cos_tab, sin_tab)


def _pair_lhs(xa, xb, low):
    l0 = jnp.where(low, xa, pltpu.roll(xb, 64, 1))
    l1 = jnp.where(low, pltpu.roll(xa, 64, 1), xb)
    return jnp.concatenate([l0, l1], axis=0).astype(BF16)


def _compress_bias(pos_ref, wpair_ref, s):
    r = jnp.zeros((16, 2 * CMP_HIDDEN), F32)
    for p in range(8):
        lhs = jnp.concatenate([jnp.broadcast_to(pos_ref[0, p:p + 1, :], (8, 128)),
                               jnp.broadcast_to(pos_ref[1, p:p + 1, :], (8, 128))], axis=0).astype(BF16)
        r = r + _dot(lhs, wpair_ref[s, p])
    return r[0:1, 0:CMP_HIDDEN] + r[8:9, CMP_HIDDEN:]


def _compress_finalize(ab_ref, pos_ref, wpair_ref, w2bd_ref, nch):
    row = lax.broadcasted_iota(jnp.int32, (nch, CMP_HIDDEN), 0)
    outs = []
    for s in range(2):
        bias = _compress_bias(pos_ref, wpair_ref, s)
        hid = []
        for g in range(NSA_KV):
            a = ab_ref[s, g * nch:(g + 1) * nch, 0:CMP_HIDDEN]
            bsh = ab_ref[s, pl.ds(g * nch + 1, nch), CMP_HIDDEN:2 * CMP_HIDDEN]
            hg = jax.nn.gelu(a + bsh + bias)
            hid.append(jnp.where(row < nch - 1, hg, 0.0))
        outs.append(_dot(jnp.concatenate(hid, axis=1).astype(BF16), w2bd_ref[s]))
    return outs[0], outs[1]


def _compress_prompt_kernel(rows_k_ref, rows_v_ref, wpair_ref, pos_ref, w2bd_ref, kc2_ref, vcT_ref, ab_ref, *, nch):
    lane = lax.broadcasted_iota(jnp.int32, (nch, V7X_LANES), 1)
    low = lane < HEAD_DIM
    ab_ref[:, 2 * nch:2 * nch + 8, :] = jnp.zeros((2, 8, 2 * CMP_HIDDEN), F32)
    for s, rows_ref in enumerate((rows_k_ref, rows_v_ref)):
        acc = jnp.zeros((2 * nch, 2 * CMP_HIDDEN), F32)
        for p in range(8):
            xa = rows_ref[pl.ds(2 * p, nch, stride=CMP_STRIDE), :]
            xb = rows_ref[pl.ds(2 * p + 1, nch, stride=CMP_STRIDE), :]
            acc = acc + _dot(_pair_lhs(xa, xb, low), wpair_ref[s, p])
        ab_ref[s, 0:2 * nch, :] = acc
    kc, vc = _compress_finalize(ab_ref, pos_ref, wpair_ref, w2bd_ref, nch)
    kc2_ref[0] = kc.astype(BF16)
    kc2_ref[1] = pltpu.roll(kc, 64, 1).astype(BF16)
    vcT_ref[...] = vc.T.astype(BF16)


def _compress_prompt(cmp_rows, wpair, pospair, w2bd, *, b, s):
    nch = s // CMP_STRIDE
    return pl.pallas_call(
        functools.partial(_compress_prompt_kernel, nch=nch),
        grid=(b,),
        in_specs=[
            pl.BlockSpec((s, 128), lambda i: (i, 0)),
            pl.BlockSpec((s, 128), lambda i: (i, 1)),
            pl.BlockSpec(wpair.shape, lambda i: (0, 0, 0, 0)),
            pl.BlockSpec(pospair.shape, lambda i: (0, 0, 0)),
            pl.BlockSpec(w2bd.shape, lambda i: (0, 0, 0)),
        ],
        out_specs=[
            pl.BlockSpec((None, 2, nch, 128), lambda i: (i, 0, 0, 0)),
            pl.BlockSpec((None, 128, nch), lambda i: (i, 0, 0)),
        ],
        out_shape=[jax.ShapeDtypeStruct((b, 2, nch, 128), BF16),
                   jax.ShapeDtypeStruct((b, 128, nch), BF16)],
        scratch_shapes=[pltpu.VMEM((2, 2 * nch + 8, 2 * CMP_HIDDEN), F32)],
        compiler_params=_cparams(("parallel",), 40 << 20),
        name="compress_prompt",
    )(cmp_rows, cmp_rows, wpair, pospair, w2bd)


def _flash_pipeline(nkt, nchain, qk_fn, v_fn, valid_fn, s_refs, p_ref, al_ref, m_ref, acc_ref, tm_ref, sub=128):
    tk = p_ref.shape[1]
    m_ref[...] = jnp.full(m_ref.shape, NEG, F32)
    acc_ref[...] = jnp.zeros(acc_ref.shape, F32)
    p_ref[...] = jnp.zeros(p_ref.shape, BF16)
    al_ref[...] = jnp.ones(al_ref.shape, F32)
    def score(c, kt, dst):
        st = qk_fn(c, kt)
        dst[c] = st
        tm_ref[c] = jnp.max(st, axis=0, keepdims=True)

    for c in range(nchain):
        score(c, 0, s_refs[0])

    def step(i, cur, nxt, masked, do_qk):
        iprev = jnp.maximum(i - 1, 0)
        for c in range(nchain):
            acc_ref[c] = al_ref[c] * acc_ref[c] + _dot(v_fn(c, iprev), p_ref[c])
        tile_max = [tm_ref[c] for c in range(nchain)]
        if do_qk:
            for c in range(nchain):
                score(c, i + 1, nxt)
        for c in range(nchain):
            m_old = m_ref[c]
            if masked:
                m_new = m_old
                for r in range(0, tk, sub):
                    st = jnp.where(valid_fn(i, r, sub), cur[c, r:r + sub, :], NEG)
                    cur[c, r:r + sub, :] = st
                    m_new = jnp.maximum(m_new, jnp.max(st, axis=0, keepdims=True))
            else:
                m_new = jnp.maximum(m_old, tile_max[c])
            al_ref[c] = jnp.exp2(m_old - m_new)
            m_ref[c] = m_new
            for r in range(0, tk, sub):
                p_ref[c, r:r + sub, :] = jnp.exp2(cur[c, r:r + sub, :] - m_new).astype(BF16)

    s0, s1 = s_refs
    npair = (nkt - 1) // 2

    def body(j, carry):
        step(2 * j, s0, s1, False, True)
        step(2 * j + 1, s1, s0, False, True)
        return carry

    lax.fori_loop(0, npair, body, 0)

    @pl.when(nkt - 1 - 2 * npair == 1)
    def _():
        step(nkt - 2, s0, s1, False, True)
        step(nkt - 1, s1, s0, True, False)

    @pl.when(nkt - 1 - 2 * npair == 0)
    def _():
        step(nkt - 1, s0, s1, True, False)

    for c in range(nchain):
        acc_ref[c] = al_ref[c] * acc_ref[c] + _dot(v_fn(c, nkt - 1), p_ref[c])


def _rank_blocks_T(e_ref, rank_ref, jmax):
    nv = 128 // 8
    srow = lax.broadcasted_iota(jnp.int32, (8, 128), 0)
    rank_ref[...] = jnp.zeros((128, 128), F32)
    for c in range(nv):
        @pl.when(8 * c <= jmax)
        def _():
            ec = e_ref[8 * c:8 * c + 8, :]
            rows = [jnp.broadcast_to(ec[r:r + 1, :], (8, 128)) for r in range(8)]
            for v in range(nv):
                ev = e_ref[8 * v:8 * v + 8, :]
                cnt = jnp.zeros((8, 128), F32)
                for r in range(8):
                    if v > c:
                        cnt = cnt + jnp.where(rows[r] >= ev, 1.0, 0.0)
                    elif v < c:
                        cnt = cnt + jnp.where(rows[r] > ev, 1.0, 0.0)
                    else:
                        cnt = cnt + jnp.where(srow > r, jnp.where(rows[r] >= ev, 1.0, 0.0),
                                              jnp.where(rows[r] > ev, 1.0, 0.0))
                rank_ref[8 * v:8 * v + 8, :] += cnt


def _nsa_prompt_kernel(qc_ref, qr_ref, gate_ref, kc2_ref, vcT_ref, ovlT_ref, kaug_ref, vslcT_ref,
                       kwin_ref, vwinT_ref, o_ref, e_ref, rank_ref, m_ref, acc_ref, ow_ref, s0_ref, s1_ref, p_ref, al_ref,
                       tm_ref, *, tq, tk, nch):
    qb = pl.program_id(1)
    t0 = qb * tq
    ncol = NSA_REP * tq
    hcol = ncol // 2
    col = lax.broadcasted_iota(jnp.int32, (1, ncol), 1)
    qcol = col & (tq - 1)
    tcol = t0 + qcol
    cidx = lax.broadcasted_iota(jnp.int32, (nch, ncol), 0)
    cvalid = (cidx * CMP_STRIDE + (CMP_BLOCK - 1)) <= tcol
    jrow = lax.broadcasted_iota(jnp.int32, (128, tq), 0)
    cur = (t0 + lax.broadcasted_iota(jnp.int32, (128, tq), 1)) >> 6
    jmax = (t0 + tq - 1) >> 6
    nkt = (t0 + tq - 1) // tk + 1
    nwin = WINDOW // tq
    gT = gate_ref[...].T

    ocT = []
    lhs = {}
    sc_raw = [_dot_nt(kc2_ref[g], jnp.concatenate(
        [qc_ref[:, h * 128:(h + 1) * 128] for h in range(g * NSA_REP, (g + 1) * NSA_REP)], axis=0))
        for g in range(NSA_KV)]
    for g in range(NSA_KV):
        heads = [g * NSA_REP + r for r in range(NSA_REP)]
        sc = jnp.where(cvalid, sc_raw[g], NEG)
        ec = jnp.exp2(sc - jnp.max(sc, axis=0, keepdims=True))
        lc = jnp.sum(ec, axis=0, keepdims=True)
        p = ec * jnp.where(tcol >= CMP_BLOCK - 1, 1.0 / lc, 0.0)
        ocT.append(_dot(vcT_ref[g * HEAD_DIM:(g + 1) * HEAD_DIM, :], p.astype(BF16)))
        ps = p[:, 0:tq] + p[:, tq:2 * tq] + p[:, 2 * tq:3 * tq] + p[:, 3 * tq:4 * tq]
        hi = ps.astype(BF16)
        lo = (ps - hi.astype(F32)).astype(BF16)
        impT = _dot(ovlT_ref[...], hi) + _dot(ovlT_ref[...], lo)
        forced = (jrow == 0) | (jrow == cur) | (jrow == cur - 1)
        e_ref[...] = jnp.where(jrow > cur, -1.0, jnp.where(forced, FORCED, impT))
        _rank_blocks_T(e_ref, rank_ref, jmax)
        selT = jnp.where(jrow <= cur, jnp.where(rank_ref[...] < float(SEL_TOPK), 0.0, NEG), NEG)
        self_g = selT.T.astype(BF16)
        lhs[g] = jnp.concatenate(
            [jnp.concatenate([qr_ref[:, h * 128:(h + 1) * 128], self_g], axis=1) for h in heads], axis=0)

    def sel_qk(g, kt):
        k0 = pl.multiple_of(kt * tk, tk)
        return _dot_nt(kaug_ref[pl.ds(k0, tk), g * 256:(g + 1) * 256], lhs[g])

    def sel_valid(kt, r, n):
        return (kt * tk + r + lax.broadcasted_iota(jnp.int32, (n, ncol), 0)) <= tcol

    _flash_pipeline(nkt, NSA_KV, sel_qk, lambda g, kt: vslcT_ref[kt, g], sel_valid,
                    (s0_ref, s1_ref), p_ref, al_ref, m_ref, acc_ref, tm_ref)

    @pl.when(qb >= nwin)
    def _():
        ii = lax.broadcasted_iota(jnp.int32, (tq, ncol), 0)
        chunks = {}
        for g in range(NSA_KV):
            qr_g = jnp.concatenate([qr_ref[:, h * 128:(h + 1) * 128]
                                    for h in range(g * NSA_REP, (g + 1) * NSA_REP)], axis=0)
            for c in range(nwin + 1):
                kst = pl.multiple_of(t0 - WINDOW + c * tq, tq)
                chunks[g, c] = _dot_nt(kwin_ref[pl.ds(kst, tq), g * 128:(g + 1) * 128], qr_g)
        ews = {}
        lws = []
        for g in range(NSA_KV):
            chunks[g, 0] = jnp.where(ii > qcol, chunks[g, 0], NEG)
            chunks[g, nwin] = jnp.where(ii <= qcol, chunks[g, nwin], NEG)
            m = jnp.max(functools.reduce(jnp.maximum, [chunks[g, c] for c in range(nwin + 1)]), axis=0, keepdims=True)
            lw = jnp.zeros((1, ncol), F32)
            for c in range(nwin + 1):
                ew = jnp.exp2(chunks[g, c] - m)
                lw = lw + jnp.sum(ew, axis=0, keepdims=True)
                ews[g, c] = ew.astype(BF16)
            lws.append(lw)
        for g in range(NSA_KV):
            acc = jnp.zeros((HEAD_DIM, ncol), F32)
            for c in range(nwin + 1):
                acc = acc + _dot(vwinT_ref[qb - nwin + c, g * HEAD_DIM:(g + 1) * HEAD_DIM, :], ews[g, c])
            ow_ref[g] = acc * (1.0 / lws[g])

    @pl.when(qb < nwin)
    def _():
        nwk = WINDOW + tq
        for g in range(NSA_KV):
            qr_g = jnp.concatenate([qr_ref[:, h * 128:(h + 1) * 128]
                                    for h in range(g * NSA_REP, (g + 1) * NSA_REP)], axis=0)
            sw = _dot_nt(kwin_ref[0:nwk, g * 128:(g + 1) * 128], qr_g)
            dist = tcol - lax.broadcasted_iota(jnp.int32, (nwk, ncol), 0)
            sw = jnp.where(dist >= 0, sw, NEG)
            ew = jnp.exp2(sw - jnp.max(sw, axis=0, keepdims=True))
            lw = jnp.sum(ew, axis=0, keepdims=True)
            ewb = ew.astype(BF16)
            acc = jnp.zeros((HEAD_DIM, ncol), F32)
            for i in range(nwk // 128):
                acc = acc + _dot(vwinT_ref[i, g * HEAD_DIM:(g + 1) * HEAD_DIM, :], ewb[i * 128:(i + 1) * 128, :])
            ow_ref[g] = acc * (1.0 / lw)

    for g in range(NSA_KV):
        owT = ow_ref[g]
        acc = acc_ref[g]
        osT_g = acc[0:HEAD_DIM] * (1.0 / acc[HEAD_DIM:HEAD_DIM + 1])
        for half in range(2):
            osT = osT_g[:, half * hcol:(half + 1) * hcol]
            parts = []
            for k, r in enumerate((2 * half, 2 * half + 1)):
                h = g * NSA_REP + r
                parts.append(gT[3 * h:3 * h + 1, :] * ocT[g][:, r * tq:(r + 1) * tq]
                             + gT[3 * h + 1:3 * h + 2, :] * osT[:, k * tq:(k + 1) * tq]
                             + gT[3 * h + 2:3 * h + 3, :] * owT[:, r * tq:(r + 1) * tq])
            ch = g * (NSA_REP // 2) + half
            o_ref[:, ch * 128:(ch + 1) * 128] = jnp.concatenate(parts, axis=0).T


def _nsa_prompt(qc, qr, gates, kc2, vcT, ovlT, kaug, vslcT, kwin, vwinT, *, b, s):
    tq, tk = Q_TILE, K_TILE
    nqb = s // tq
    nch = s // CMP_STRIDE
    assert s >= WINDOW + tq and s // SEL_BLOCK <= 128 and s % tk == 0 and tk % tq == 0
    qspec = lambda w: pl.BlockSpec((tq, w), lambda i, j: (i * nqb + j, 0))
    vrows = HEAD_DIM + ONES_ROWS
    ncol = NSA_REP * tq
    return pl.pallas_call(
        functools.partial(_nsa_prompt_kernel, tq=tq, tk=tk, nch=nch),
        grid=(b, nqb),
        in_specs=[
            qspec(1024), qspec(1024), qspec(128),
            pl.BlockSpec((None, 2, nch, 128), lambda i, j: (i, 0, 0, 0)),
            pl.BlockSpec((None, 128, nch), lambda i, j: (i, 0, 0)),
            pl.BlockSpec(ovlT.shape, lambda i, j: (0, 0)),
            pl.BlockSpec((s, 512), lambda i, j: (i, 0)),
            pl.BlockSpec((None, s // tk, NSA_KV, vrows, tk), lambda i, j: (i, 0, 0, 0, 0)),
            pl.BlockSpec((s, 256), lambda i, j: (i, 0)),
            pl.BlockSpec((None, s // 128, 128, 128), lambda i, j: (i, 0, 0, 0)),
        ],
        out_specs=qspec(512),
        out_shape=jax.ShapeDtypeStruct((b * s, 512), F32),
        scratch_shapes=[pltpu.VMEM((128, 128), F32), pltpu.VMEM((128, 128), F32),
                        pltpu.VMEM((NSA_KV, 1, ncol), F32),
                        pltpu.VMEM((NSA_KV, vrows, ncol), F32),
                        pltpu.VMEM((NSA_KV, HEAD_DIM, ncol), F32),
                        pltpu.VMEM((NSA_KV, tk, ncol), F32), pltpu.VMEM((NSA_KV, tk, ncol), F32),
                        pltpu.VMEM((NSA_KV, tk, ncol), BF16), pltpu.VMEM((NSA_KV, 1, ncol), F32),
                        pltpu.VMEM((NSA_KV, 1, ncol), F32)],
        compiler_params=_cparams(("parallel", "arbitrary"), 54 << 20),
        name="nsa_prompt",
    )(qc, qr, gates, kc2, vcT, ovlT, kaug, vslcT, kwin, vwinT)


def _lambda(lq1_ref, lk1_ref, lq2_ref, lk2_ref, lam_init):
    a = jnp.sum(lq1_ref[...] * lk1_ref[...], axis=-1, keepdims=True)
    b = jnp.sum(lq2_ref[...] * lk2_ref[...], axis=-1, keepdims=True)
    return jnp.exp(a) - jnp.exp(b) + lam_init


def _diff_prompt_kernel(dq_ref, dk_ref, dvT_ref, lq1_ref, lk1_ref, lq2_ref, lk2_ref, o_ref, m_ref, acc_ref,
                        s0_ref, s1_ref, p_ref, al_ref, tm_ref, *, tq, tk, lam_init):
    qb = pl.program_id(1)
    t0 = qb * tq
    ncol = 2 * tq
    tcol = t0 + (lax.broadcasted_iota(jnp.int32, (1, ncol), 1) & (tq - 1))
    nkt = (t0 + tq - 1) // tk + 1
    low = lax.broadcasted_iota(jnp.int32, (tq, V7X_LANES), 1) < DIFF_QK
    zero = jnp.zeros((tq, V7X_LANES), BF16)
    qbd = []
    for h in range(DIFF_HEADS):
        dqh = dq_ref[:, h * 128:(h + 1) * 128]
        qbd.append(jnp.concatenate([jnp.where(low, dqh, zero), jnp.where(low, zero, dqh)], axis=0))

    def qk(h, kt):
        k0 = pl.multiple_of(kt * tk, tk)
        return _dot_nt(dk_ref[pl.ds(k0, tk), h * 128:(h + 1) * 128], qbd[h])

    def valid(kt, r, n):
        return (kt * tk + r + lax.broadcasted_iota(jnp.int32, (n, ncol), 0)) <= tcol

    _flash_pipeline(nkt, DIFF_HEADS, qk, lambda h, kt: dvT_ref[kt, h], valid,
                    (s0_ref, s1_ref), p_ref, al_ref, m_ref, acc_ref, tm_ref)
    lam = _lambda(lq1_ref, lk1_ref, lq2_ref, lk2_ref, lam_init)
    for h in range(DIFF_HEADS):
        acc = acc_ref[h]
        o = acc[0:DIFF_V] * (1.0 / acc[DIFF_V:DIFF_V + 1])
        o_ref[:, h * DIFF_V:(h + 1) * DIFF_V] = (o[:, 0:tq] - lam * o[:, tq:2 * tq]).T


def _diff_prompt(dq, dk, dvT, lams, *, b, s, lam_init):
    tq, tk = DQ_TILE, K_TILE
    assert s % tq == 0 and tk % tq == 0
    nqb = s // tq
    vrows = DIFF_V + ONES_ROWS
    ncol = 2 * tq
    lspec = pl.BlockSpec((1, DIFF_QK), lambda i, j: (0, 0))
    return pl.pallas_call(
        functools.partial(_diff_prompt_kernel, tq=tq, tk=tk, lam_init=lam_init),
        grid=(b, nqb),
        in_specs=[
            pl.BlockSpec((tq, 512), lambda i, j: (i * nqb + j, 0)),
            pl.BlockSpec((s, 512), lambda i, j: (i, 0)),
            pl.BlockSpec((None, s // tk, DIFF_HEADS, vrows, tk), lambda i, j: (i, 0, 0, 0, 0)),
            lspec, lspec, lspec, lspec,
        ],
        out_specs=pl.BlockSpec((tq, 512), lambda i, j: (i * nqb + j, 0)),
        out_shape=jax.ShapeDtypeStruct((b * s, 512), F32),
        scratch_shapes=[pltpu.VMEM((DIFF_HEADS, 1, ncol), F32), pltpu.VMEM((DIFF_HEADS, vrows, ncol), F32),
                        pltpu.VMEM((DIFF_HEADS, tk, ncol), F32), pltpu.VMEM((DIFF_HEADS, tk, ncol), F32),
                        pltpu.VMEM((DIFF_HEADS, tk, ncol), BF16), pltpu.VMEM((DIFF_HEADS, 1, ncol), F32),
                        pltpu.VMEM((DIFF_HEADS, 1, ncol), F32)],
        compiler_params=_cparams(("parallel", "arbitrary"), 54 << 20),
        name="diff_prompt",
    )(dq, dk, dvT, *lams)


def _page_specs(pp, rows, row_block):
    return [pl.BlockSpec((None, rows, 128), functools.partial(
        lambda b, c, pt, i: (pt[b, c * pp + i], row_block, 0), i=i)) for i in range(pp)]


def _s1_kernel(pt_ref, *refs, pp, nchk, nj):
    pages = refs[:pp]
    (qc_ref, wpair_ref, pos_ref, w2bd_ref, ovl_ref, perm_ref, wsplit_ref,
     oc_ref, self_ref, ab_ref, y_ref) = refs[pp:]
    c = pl.program_id(1)
    mrows = pp * 8

    @pl.when(c == 0)
    def _():
        ab_ref[:, 2 * nchk:2 * nchk + 8, :] = jnp.zeros((2, 8, 2 * CMP_HIDDEN), F32)

    for i, pg in enumerate(pages):
        y = _dot_nt(perm_ref[...], pg[...].astype(BF16))
        for t in range(CMP_STRIDE):
            y_ref[t, i * 8:(i + 1) * 8, :] = y[t * 8:(t + 1) * 8, :]
    r0 = pl.multiple_of(c * mrows, 8)
    for s in range(2):
        acc = jnp.zeros((mrows, 4 * CMP_HIDDEN), F32)
        for p in range(CMP_STRIDE // 2):
            x = jnp.concatenate([y_ref[2 * p, :, s * 128:(s + 1) * 128],
                                 y_ref[2 * p + 1, :, s * 128:(s + 1) * 128]], axis=1).astype(BF16)
            acc = acc + _dot(x, wsplit_ref[s, p])
        ab_ref[s, pl.ds(r0, mrows), :] = acc[:, 0:2 * CMP_HIDDEN]
        ab_ref[s, pl.ds(nchk + r0, mrows), :] = acc[:, 2 * CMP_HIDDEN:4 * CMP_HIDDEN]

    @pl.when(c == pl.num_programs(1) - 1)
    def _():
        kc, vc = _compress_finalize(ab_ref, pos_ref, wpair_ref, w2bd_ref, nchk)
        kcb = kc.astype(BF16)
        vcb = vc.astype(BF16)
        nrow = NSA_REP * DEC_ROWS
        lane = lax.broadcasted_iota(jnp.int32, (DEC_ROWS, V7X_LANES), 1)
        low = lane < HEAD_DIM
        past = nchk * CMP_STRIDE
        trow = past + (lax.broadcasted_iota(jnp.int32, (nrow, nchk), 0) & (DEC_ROWS - 1))
        cvalid = (lax.broadcasted_iota(jnp.int32, (nrow, nchk), 1) * CMP_STRIDE + (CMP_BLOCK - 1)) <= trow
        for g in range(NSA_KV):
            heads = [g * NSA_REP + r for r in range(NSA_REP)]
            q = jnp.concatenate([qc_ref[:, h * 128:(h + 1) * 128] for h in heads], axis=0).astype(F32)
            if g == 1:
                q = pltpu.roll(q, 64, 1)
            s_c = jnp.where(cvalid, _dot_nt(q.astype(BF16), kcb), NEG)
            m = jnp.max(s_c, axis=1, keepdims=True)
            e = jnp.where(cvalid, jnp.exp2(s_c - m), 0.0)
            l = jnp.sum(e, axis=1, keepdims=True)
            p = e * (1.0 / jnp.where(l > 0.0, l, 1.0))
            o = _dot(p.astype(BF16), vcb)
            for mpair in range(NSA_REP // 2):
                ev = o[(2 * mpair) * DEC_ROWS:(2 * mpair + 1) * DEC_ROWS, :]
                od = o[(2 * mpair + 1) * DEC_ROWS:(2 * mpair + 2) * DEC_ROWS, :]
                pair = jnp.where(low, ev, pltpu.roll(od, 64, 1)) if g == 0 else jnp.where(low, pltpu.roll(ev, 64, 1), od)
                ch = g * (NSA_REP // 2) + mpair
                oc_ref[:, ch * 128:(ch + 1) * 128] = pair
            ps = p[0:DEC_ROWS] + p[DEC_ROWS:2 * DEC_ROWS] + p[2 * DEC_ROWS:3 * DEC_ROWS] + p[3 * DEC_ROWS:4 * DEC_ROWS]
            hi = ps.astype(BF16)
            lo = (ps - hi.astype(F32)).astype(BF16)
            imp = _dot(hi, ovl_ref[...]) + _dot(lo, ovl_ref[...])
            ev = jnp.where(lane >= nj, -1.0, jnp.where((lane == 0) | (lane == nj - 1), FORCED, imp))
            rank = jnp.zeros((DEC_ROWS, V7X_LANES), F32)
            for jp in range(nj):
                cb = jnp.broadcast_to(ev[:, jp:jp + 1], (DEC_ROWS, V7X_LANES))
                rank = rank + jnp.where(lane > jp, jnp.where(cb >= ev, 1.0, 0.0), jnp.where(cb > ev, 1.0, 0.0))
            self_ref[g] = jnp.where(lane < nj, jnp.where(rank < float(SEL_TOPK - 1), 0.0, NEG), NEG)


def _s1(page_table, cache_nsaT, qc_s, wpair, pospair, w2bd, ovl, perm, wsplit, *, nseq, npages):
    pp = min(16, npages)
    assert npages % pp == 0
    nchk = npages * 128 // CMP_STRIDE
    nj = npages * 128 // SEL_BLOCK
    assert nj <= 128
    cmap3 = lambda b, c, pt: (0, 0, 0)
    grid_spec = pltpu.PrefetchScalarGridSpec(
        num_scalar_prefetch=1,
        grid=(nseq, npages // pp),
        in_specs=_page_specs(pp, 256, 0) + [
            pl.BlockSpec((None, DEC_ROWS, 1024), lambda b, c, pt: (b, 0, 0)),
            pl.BlockSpec(wpair.shape, lambda b, c, pt: (0, 0, 0, 0)),
            pl.BlockSpec(pospair.shape, cmap3),
            pl.BlockSpec(w2bd.shape, cmap3),
            pl.BlockSpec(ovl.shape, lambda b, c, pt: (0, 0)),
            pl.BlockSpec(perm.shape, lambda b, c, pt: (0, 0)),
            pl.BlockSpec(wsplit.shape, lambda b, c, pt: (0, 0, 0, 0)),
        ],
        out_specs=[
            pl.BlockSpec((None, DEC_ROWS, 512), lambda b, c, pt: (b, 0, 0)),
            pl.BlockSpec((None, 2, DEC_ROWS, 128), lambda b, c, pt: (b, 0, 0, 0)),
        ],
        scratch_shapes=[pltpu.VMEM((2, 2 * nchk + 8, 2 * CMP_HIDDEN), F32),
                        pltpu.VMEM((CMP_STRIDE, pp * 8, 256), F32)],
    )
    return pl.pallas_call(
        functools.partial(_s1_kernel, pp=pp, nchk=nchk, nj=nj),
        grid_spec=grid_spec,
        out_shape=[jax.ShapeDtypeStruct((nseq, DEC_ROWS, 512), F32),
                   jax.ShapeDtypeStruct((nseq, 2, DEC_ROWS, 128), F32)],
        compiler_params=_cparams(("parallel", "arbitrary"), 32 << 20),
        name="sample_compress_select",
    )(page_table, *([cache_nsaT] * pp), qc_s, wpair, pospair, w2bd, ovl, perm, wsplit)


def _rows_both_groups(q_ref):
    blocks = []
    for g in range(NSA_KV):
        qg = jnp.concatenate([q_ref[:, h * 128:(h + 1) * 128] for h in range(g * NSA_REP, (g + 1) * NSA_REP)], axis=0)
        if g == 1:
            qg = pltpu.roll(qg.astype(F32), 64, 1).astype(BF16)
        blocks.append(qg)
    return jnp.concatenate(blocks, axis=0)


def _s2_kernel(pt_ref, *refs, pp, past):
    pages = refs[:pp]
    (qr_ref, self_ref, gate_ref, oc_ref, nsa_new_ref, win_new_ref, state_ref,
     o_ref, m_ref, l_ref, acc_ref) = refs[pp:]
    c = pl.program_id(1)
    nrow = NSA_KV * NSA_REP * DEC_ROWS
    qall = _rows_both_groups(qr_ref)
    qrow = lax.broadcasted_iota(jnp.int32, (nrow, V7X_LANES), 0) & (DEC_ROWS - 1)
    lane = lax.broadcasted_iota(jnp.int32, (nrow, V7X_LANES), 1)

    @pl.when(c == 0)
    def _():
        m_ref[...] = jnp.full((nrow, 1), NEG, F32)
        l_ref[...] = jnp.zeros((nrow, 1), F32)
        acc_ref[...] = jnp.zeros((nrow, V7X_LANES), F32)

    sel_rows = jnp.concatenate([self_ref[g] for g in range(NSA_KV) for _ in range(NSA_REP)], axis=0).astype(BF16)
    nkeys = pp * 128
    blk = (c * nkeys + lax.broadcasted_iota(jnp.int32, (V7X_LANES, nkeys), 1)) >> 6
    expand = jnp.where(lax.broadcasted_iota(jnp.int32, (V7X_LANES, nkeys), 0) == blk, 1.0, 0.0).astype(BF16)
    s = jnp.concatenate([_dot(qall, pg[0:128, :].astype(BF16)) for pg in pages], axis=1) + _dot(sel_rows, expand)
    m_new = jnp.maximum(m_ref[...], jnp.max(s, axis=1, keepdims=True))
    alpha = jnp.exp2(m_ref[...] - m_new)
    p = jnp.exp2(s - m_new)
    l_ref[...] = alpha * l_ref[...] + jnp.sum(p, axis=1, keepdims=True)
    pv = jnp.zeros((nrow, V7X_LANES), F32)
    for i, pg in enumerate(pages):
        pv = pv + _dot_nt(p[:, i * 128:(i + 1) * 128].astype(BF16), pg[128:256, :].astype(BF16))
    acc_ref[...] = alpha * acc_ref[...] + pv
    m_ref[...] = m_new

    @pl.when(c == pl.num_programs(1) - 1)
    def _():
        zpad = jnp.zeros((V7X_LANES - DEC_ROWS, V7X_LANES), BF16)
        newvalid = lane <= qrow
        kn = jnp.concatenate([nsa_new_ref[:, 256:384].astype(BF16), zpad], axis=0)
        vn = jnp.concatenate([nsa_new_ref[:, 384:512].astype(BF16), zpad], axis=0)
        sn = jnp.where(newvalid, _dot_nt(qall, kn), NEG)
        m2 = jnp.maximum(m_ref[...], jnp.max(sn, axis=1, keepdims=True))
        a2 = jnp.exp2(m_ref[...] - m2)
        pn = jnp.exp2(sn - m2)
        l2 = a2 * l_ref[...] + jnp.sum(pn, axis=1, keepdims=True)
        o_s = (a2 * acc_ref[...] + _dot(pn.astype(BF16), vn)) * (1.0 / l2)
        wb = state_ref.shape[1]
        sw = _dot(qall, state_ref[0:128, :].astype(BF16))
        qrow_w = lax.broadcasted_iota(jnp.int32, (nrow, wb), 0) & (DEC_ROWS - 1)
        dist = (past + qrow_w) - (past - wb + lax.broadcasted_iota(jnp.int32, (nrow, wb), 1))
        sw = jnp.where(dist < WINDOW, sw, NEG)
        kwn = jnp.concatenate([win_new_ref[:, 0:128].astype(BF16), zpad], axis=0)
        vwn = jnp.concatenate([win_new_ref[:, 128:256].astype(BF16), zpad], axis=0)
        swn = jnp.where(newvalid, _dot_nt(qall, kwn), NEG)
        mw = jnp.maximum(jnp.max(sw, axis=1, keepdims=True), jnp.max(swn, axis=1, keepdims=True))
        pw = jnp.exp2(sw - mw)
        pwn = jnp.exp2(swn - mw)
        lw = jnp.sum(pw, axis=1, keepdims=True) + jnp.sum(pwn, axis=1, keepdims=True)
        o_w = (_dot_nt(pw.astype(BF16), state_ref[128:256, :].astype(BF16)) + _dot(pwn.astype(BF16), vwn)) * (1.0 / lw)
        low = lane[0:DEC_ROWS] < HEAD_DIM
        gates = gate_ref[...]

        def pair_rows(x, g, mpair):
            base = g * NSA_REP * DEC_ROWS
            ev = x[base + (2 * mpair) * DEC_ROWS:base + (2 * mpair + 1) * DEC_ROWS, :]
            od = x[base + (2 * mpair + 1) * DEC_ROWS:base + (2 * mpair + 2) * DEC_ROWS, :]
            if g == 0:
                return jnp.where(low, ev, pltpu.roll(od, 64, 1))
            return jnp.where(low, pltpu.roll(ev, 64, 1), od)

        for g in range(NSA_KV):
            for mpair in range(NSA_REP // 2):
                ch = g * (NSA_REP // 2) + mpair
                he, ho = 2 * ch, 2 * ch + 1
                out = jnp.zeros((DEC_ROWS, V7X_LANES), F32)
                branches = (oc_ref[:, ch * 128:(ch + 1) * 128], pair_rows(o_s, g, mpair), pair_rows(o_w, g, mpair))
                for i, br in enumerate(branches):
                    gcol = jnp.where(low, jnp.broadcast_to(gates[:, 3 * he + i:3 * he + i + 1], (DEC_ROWS, V7X_LANES)),
                                     jnp.broadcast_to(gates[:, 3 * ho + i:3 * ho + i + 1], (DEC_ROWS, V7X_LANES)))
                    out = out + gcol * br
                o_ref[:, ch * 128:(ch + 1) * 128] = out


def _s2(page_table, cache_nsaT, qr_s, selfeat, gates_s, oc_s, nsa_new, win_new, state_t, *, nseq, npages):
    pp = min(16, npages)
    assert npages % pp == 0
    wb = state_t.shape[2]
    past = npages * 128
    assert wb == WINDOW and past >= wb
    nrow = NSA_KV * NSA_REP * DEC_ROWS
    per_seq = lambda *shape: pl.BlockSpec((None,) + shape, lambda b, c, pt: (b,) + (0,) * len(shape))
    grid_spec = pltpu.PrefetchScalarGridSpec(
        num_scalar_prefetch=1,
        grid=(nseq, npages // pp),
        in_specs=_page_specs(pp, 256, 1) + [
            per_seq(DEC_ROWS, 1024), per_seq(2, DEC_ROWS, 128), per_seq(DEC_ROWS, 128), per_seq(DEC_ROWS, 512),
            per_seq(DEC_ROWS, 512), per_seq(DEC_ROWS, 256), per_seq(256, wb),
        ],
        out_specs=per_seq(DEC_ROWS, 512),
        scratch_shapes=[pltpu.VMEM((nrow, 1), F32), pltpu.VMEM((nrow, 1), F32), pltpu.VMEM((nrow, V7X_LANES), F32)],
    )
    return pl.pallas_call(
        functools.partial(_s2_kernel, pp=pp, past=past),
        grid_spec=grid_spec,
        out_shape=jax.ShapeDtypeStruct((nseq, DEC_ROWS, 512), F32),
        compiler_params=_cparams(("parallel", "arbitrary"), 32 << 20),
        name="sample_select_window",
    )(page_table, *([cache_nsaT] * pp), qr_s, selfeat, gates_s, oc_s, nsa_new, win_new, state_t)


def _sdiff_kernel(pt_ref, *refs, pp, lam_init):
    pages = refs[:pp]
    dq_ref, new_ref, lq1_ref, lk1_ref, lq2_ref, lk2_ref, o_ref, m_ref, l_ref, acc_ref = refs[pp:]
    c = pl.program_id(1)
    hrow = 2 * DEC_ROWS
    nrow = DIFF_HEADS * hrow
    lane8 = lax.broadcasted_iota(jnp.int32, (DEC_ROWS, V7X_LANES), 1)
    zero8 = jnp.zeros((DEC_ROWS, V7X_LANES), BF16)
    qh = []
    for h in range(DIFF_HEADS):
        dqh = dq_ref[:, h * 128:(h + 1) * 128]
        qh.append(jnp.concatenate([jnp.where(lane8 < DIFF_QK, dqh, zero8), jnp.where(lane8 < DIFF_QK, zero8, dqh)], axis=0))

    @pl.when(c == 0)
    def _():
        m_ref[...] = jnp.full((nrow, 1), NEG, F32)
        l_ref[...] = jnp.zeros((nrow, 1), F32)
        acc_ref[...] = jnp.zeros((nrow, DIFF_V), F32)

    def keys(pg, h):
        return pg[pl.ds(h, 128, stride=2 * DIFF_HEADS), :].astype(BF16)

    def vals(pg, h):
        return pg[pl.ds(DIFF_HEADS + h, 128, stride=2 * DIFF_HEADS), :].astype(BF16)

    s = jnp.concatenate(
        [jnp.concatenate([_dot_nt(qh[h], keys(pg, h)) for pg in pages], axis=1) for h in range(DIFF_HEADS)], axis=0)
    m_new = jnp.maximum(m_ref[...], jnp.max(s, axis=1, keepdims=True))
    alpha = jnp.exp2(m_ref[...] - m_new)
    p = jnp.exp2(s - m_new)
    l_ref[...] = alpha * l_ref[...] + jnp.sum(p, axis=1, keepdims=True)
    pvs = []
    for h in range(DIFF_HEADS):
        pv = jnp.zeros((hrow, DIFF_V), F32)
        for i, pg in enumerate(pages):
            pv = pv + _dot(p[h * hrow:(h + 1) * hrow, i * 128:(i + 1) * 128].astype(BF16), vals(pg, h))
        pvs.append(pv)
    acc_ref[...] = alpha * acc_ref[...] + jnp.concatenate(pvs, axis=0)
    m_ref[...] = m_new

    @pl.when(c == pl.num_programs(1) - 1)
    def _():
        zpad = jnp.zeros((V7X_LANES - DEC_ROWS, V7X_LANES), BF16)
        qrow = lax.broadcasted_iota(jnp.int32, (hrow, V7X_LANES), 0) & (DEC_ROWS - 1)
        lane = lax.broadcasted_iota(jnp.int32, (hrow, V7X_LANES), 1)
        lam = _lambda(lq1_ref, lk1_ref, lq2_ref, lk2_ref, lam_init)
        for h in range(DIFF_HEADS):
            rows = slice(h * hrow, (h + 1) * hrow)
            kn = jnp.concatenate([new_ref[:, h * 128:(h + 1) * 128].astype(BF16), zpad], axis=0)
            vn = jnp.concatenate([new_ref[:, 512 + h * 128:512 + (h + 1) * 128].astype(BF16), zpad], axis=0)
            sn = jnp.where(lane <= qrow, _dot_nt(qh[h], kn), NEG)
            m1 = m_ref[rows, :]
            m2 = jnp.maximum(m1, jnp.max(sn, axis=1, keepdims=True))
            a2 = jnp.exp2(m1 - m2)
            pn = jnp.exp2(sn - m2)
            l2 = a2 * l_ref[rows, :] + jnp.sum(pn, axis=1, keepdims=True)
            o = (a2 * acc_ref[rows, :] + _dot(pn.astype(BF16), vn)) * (1.0 / l2)
            o_ref[:, h * 128:(h + 1) * 128] = o[0:DEC_ROWS] - lam * o[DEC_ROWS:hrow]


def _sdiff(page_table, cache_diff8, dq_s, diff_new, lams, *, nseq, npages, lam_init):
    pp = min(16, npages)
    assert npages % pp == 0
    nrow = DIFF_HEADS * 2 * DEC_ROWS
    lspec = pl.BlockSpec((1, DIFF_QK), lambda b, c, pt: (0, 0))
    grid_spec = pltpu.PrefetchScalarGridSpec(
        num_scalar_prefetch=1,
        grid=(nseq, npages // pp),
        in_specs=_page_specs(pp, 1024, 0) + [
            pl.BlockSpec((None, DEC_ROWS, 512), lambda b, c, pt: (b, 0, 0)),
            pl.BlockSpec((None, DEC_ROWS, 1024), lambda b, c, pt: (b, 0, 0)),
            lspec, lspec, lspec, lspec,
        ],
        out_specs=pl.BlockSpec((None, DEC_ROWS, 512), lambda b, c, pt: (b, 0, 0)),
        scratch_shapes=[pltpu.VMEM((nrow, 1), F32), pltpu.VMEM((nrow, 1), F32), pltpu.VMEM((nrow, DIFF_V), F32)],
    )
    return pl.pallas_call(
        functools.partial(_sdiff_kernel, pp=pp, lam_init=lam_init),
        grid_spec=grid_spec,
        out_shape=jax.ShapeDtypeStruct((nseq, DEC_ROWS, 512), F32),
        compiler_params=_cparams(("parallel", "arbitrary"), 40 << 20),
        name="sample_diff",
    )(page_table, *([cache_diff8] * pp), dq_s, diff_new, *lams)


def _post_kernel(x_ref, on_ref, od_ref, p_ref, dg_ref, wo_ref, gpost_ref, gfpre_ref, wup_ref, wdn_ref,
                 gfpost_ref, wple_ref, wgate_ref, gple_ref, y_ref, *, lam_init, ff_chunk):
    x = x_ref[...]
    dg = dg_ref[...] * (1.0 - lam_init)
    parts = [on_ref[...].astype(BF16)]
    for h in range(DIFF_HEADS):
        od = od_ref[:, h * DIFF_V:(h + 1) * DIFF_V]
        od = od * lax.rsqrt(jnp.mean(od * od, axis=-1, keepdims=True) + EPS) * dg
        parts.append(od.astype(BF16))
    cat = jnp.concatenate(parts, axis=1)
    x1 = x + _rms(_dot(cat, wo_ref[...]), gpost_ref[...])
    h1 = _rms(x1, gfpre_ref[...]).astype(BF16)
    f = jnp.zeros(x.shape, F32)
    for c in range(wup_ref.shape[1] // ff_chunk):
        u = jnp.maximum(_dot(h1, wup_ref[:, c * ff_chunk:(c + 1) * ff_chunk]), 0.0)
        f = f + _dot((u * u).astype(BF16), wdn_ref[c * ff_chunk:(c + 1) * ff_chunk, :])
    x2 = x1 + _rms(f, gfpost_ref[...])
    e = _dot(p_ref[...].astype(BF16), wple_ref[...]) * jax.nn.sigmoid(_dot(x2.astype(BF16), wgate_ref[...]))
    y_ref[...] = x2 + _rms(e, gple_ref[...])


def _post(x2d, o_n, o_d, p2d, dgain, wo, gpost, gfpre, wup, wdn, gfpost, wple, wgate, gple, *, lam_init):
    t, d = x2d.shape
    tm = min(TOK_TILE, t)
    assert t % tm == 0
    row = lambda w: pl.BlockSpec((tm, w), lambda i: (i, 0))
    full = lambda a: pl.BlockSpec(a.shape, lambda i: (0,) * a.ndim)
    consts = (dgain, wo, gpost, gfpre, wup, wdn, gfpost, wple, wgate, gple)
    return pl.pallas_call(
        functools.partial(_post_kernel, lam_init=lam_init, ff_chunk=1024),
        grid=(t // tm,),
        in_specs=[row(d), row(512), row(512), row(p2d.shape[1])] + [full(a) for a in consts],
        out_specs=row(d),
        out_shape=jax.ShapeDtypeStruct((t, d), F32),
        compiler_params=_cparams(("parallel",), 54 << 20),
        name="merge_mlp_ple",
    )(x2d, o_n, o_d, p2d, *consts)


def _rope_tables(pos):
    half = HEAD_DIM // 2
    inv = ROPE_THETA ** (-jnp.arange(half, dtype=F32) / half)
    ang = pos.astype(F32)[:, None] * inv[None, :]
    c, s = jnp.cos(ang), jnp.sin(ang)
    return jnp.concatenate([c, c, c, c], axis=-1), jnp.concatenate([-s, s, -s, s], axis=-1)


def _overlap_T(nch):
    cs = np.arange(nch)[None, :] * CMP_STRIDE
    bs = np.arange(128)[:, None] * SEL_BLOCK
    ov = np.clip(np.minimum(cs + CMP_BLOCK, bs + SEL_BLOCK) - np.maximum(cs, bs), 0, None) / CMP_BLOCK
    return ov.astype(np.float32)


def _pair_weights(w1):
    w1r = w1.reshape(CMP_BLOCK, HEAD_DIM, CMP_HIDDEN)
    a = w1r[:CMP_STRIDE].reshape(8, 128, CMP_HIDDEN)
    b = w1r[CMP_STRIDE:].reshape(8, 128, CMP_HIDDEN)
    return jnp.concatenate([a, b], axis=-1)


def _split_weights(w1):
    w1r = w1.reshape(CMP_BLOCK, HEAD_DIM, CMP_HIDDEN)
    ab = jnp.concatenate([w1r[:CMP_STRIDE], w1r[CMP_STRIDE:]], axis=-1)
    z = jnp.zeros_like(ab)
    per_t = jnp.concatenate([jnp.concatenate([ab, z], axis=-1), jnp.concatenate([z, ab], axis=-1)], axis=1)
    return per_t.reshape(CMP_STRIDE // 2, 4 * HEAD_DIM, 4 * CMP_HIDDEN)


def _chunk_perm():
    r = np.arange(128)
    p = np.zeros((128, 128), np.float32)
    p[r, CMP_STRIDE * (r % 8) + r // 8] = 1.0
    return p


def _block_diag2(w2):
    z = jnp.zeros_like(w2)
    return jnp.concatenate([jnp.concatenate([w2, z], axis=1), jnp.concatenate([z, w2], axis=1)], axis=0)


def _token_last(a):
    nd = a.ndim
    t = jnp.transpose(a, (0, 1) + tuple(range(3, nd)) + (2,))
    return t.reshape(a.shape[0], a.shape[1], -1, a.shape[2])


def kernel(x_prompt, x_sample, cache_nsa, cache_diff, state_nsa_win, page_table, p_prompt, p_sample, g_mix_pre, w_in, cmp_pos, cmp_k_w1, cmp_k_w2, cmp_v_w1, cmp_v_w2, diff_lq1, diff_lk1, diff_lq2, diff_lk2, diff_gain, w_out, g_mix_post, g_ffn_pre, w_up, w_down, g_ffn_post, w_ple, w_ple_gate, g_ple):
    b, s, d = x_prompt.shape
    nseq, dec, _ = x_sample.shape
    depth, npool, page = cache_nsa.shape[0], cache_nsa.shape[1], cache_nsa.shape[2]
    npages = page_table.shape[1]
    past = npages * page
    wb = state_nsa_win.shape[2]
    assert page == 128 and dec <= DEC_ROWS and past % SEL_BLOCK == 0 and dec <= SEL_BLOCK
    assert (past + dec - CMP_BLOCK) // CMP_STRIDE + 1 == past // CMP_STRIDE - 1
    assert s % TOK_TILE == 0 and wb == WINDOW and s >= WINDOW

    cos_p, sin_p = _rope_tables(jnp.arange(s))
    pos_s = past + (jnp.arange(nseq * DEC_ROWS) % DEC_ROWS)
    cos_s, sin_s = _rope_tables(pos_s)
    ovlT = jnp.asarray(_overlap_T(s // CMP_STRIDE), BF16)
    ovl_s = jnp.asarray(_overlap_T(past // CMP_STRIDE).T, BF16)
    perm = jnp.asarray(_chunk_perm(), BF16)
    pt = page_table.astype(jnp.int32)
    cache_nsaT = _token_last(cache_nsa).reshape(depth * npool, 4 * NSA_KV * HEAD_DIM, page)
    cache_diff8 = cache_diff.reshape(depth * npool, page * 2 * DIFF_HEADS, DIFF_V)
    state_t = _token_last(state_nsa_win)

    xp = x_prompt.reshape(b * s, d)
    xs = jnp.pad(x_sample, ((0, 0), (0, DEC_ROWS - dec), (0, 0))).reshape(nseq * DEC_ROWS, d)
    outs = [[] for _ in range(6)]
    for i in range(depth):
        lam_init = 0.8 - 0.6 * math.exp(-0.3 * i)
        w = w_in[i]
        wcat = jnp.concatenate([w[:, :1280], w[:, 1304:], w[:, 1280:1304],
                                jnp.zeros((d, _C_END - _C_GATE - 24), w.dtype)], axis=1).astype(BF16)
        wpair = jnp.stack([_pair_weights(cmp_k_w1[i]), _pair_weights(cmp_v_w1[i])]).astype(BF16)
        pospair = jnp.stack([cmp_pos[i][:CMP_STRIDE].reshape(8, 128), cmp_pos[i][CMP_STRIDE:].reshape(8, 128)])
        w2bd = jnp.stack([_block_diag2(cmp_k_w2[i]), _block_diag2(cmp_v_w2[i])]).astype(BF16)
        lams = tuple(a[i].reshape(1, DIFF_QK) for a in (diff_lq1, diff_lk1, diff_lq2, diff_lk2))
        gpre = g_mix_pre[i].reshape(1, d)
        post_w = (diff_gain[i].reshape(1, DIFF_V), w_out[i].astype(BF16), g_mix_post[i].reshape(1, d),
                  g_ffn_pre[i].reshape(1, d), w_up[i].astype(BF16), w_down[i].astype(BF16),
                  g_ffn_post[i].reshape(1, d), w_ple[i].astype(BF16), w_ple_gate[i].astype(BF16),
                  g_ple[i].reshape(1, d))
        pt_i = pt + i * npool

        (qc, qr, dq, gates, nsa_t, win_rows, diff8, cmp_rows,
         kaug, vslcT, kwin, vwinT, dk, dvT) = _project(xp, gpre, wcat, cos_p, sin_p, seq_len=s, prompt=True)
        kc2, vcT = _compress_prompt(cmp_rows, wpair, pospair, w2bd, b=b, s=s)
        o_n = _nsa_prompt(qc, qr, gates, kc2, vcT, ovlT, kaug, vslcT, kwin, vwinT, b=b, s=s)
        o_d = _diff_prompt(dq, dk, dvT, lams, b=b, s=s, lam_init=lam_init)
        xp = _post(xp, o_n, o_d, p_prompt[i].reshape(b * s, -1), *post_w, lam_init=lam_init)
        outs[0].append(jnp.transpose(nsa_t.reshape(b, 4, NSA_KV, HEAD_DIM, s), (0, 4, 1, 2, 3)))
        outs[2].append(diff8.reshape(b, s, 2, DIFF_HEADS, DIFF_V))
        win_tail = lax.optimization_barrier(win_rows.reshape(b, s, 2 * NSA_KV * HEAD_DIM)[:, s - min(WINDOW, s):])
        outs[4].append(win_tail.reshape(b, min(WINDOW, s), 2, NSA_KV, HEAD_DIM))

        qc_s, qr_s, dq_s, gates_s, nsa_new, win_new, diff_new = _project(
            xs, gpre, wcat, cos_s, sin_s, seq_len=None, prompt=False)
        r3 = lambda a: a.reshape(nseq, DEC_ROWS, a.shape[-1])
        wsplit = jnp.stack([_split_weights(cmp_k_w1[i]), _split_weights(cmp_v_w1[i])]).astype(BF16)
        oc_s, selfeat = _s1(pt_i, cache_nsaT, r3(qc_s), wpair, pospair, w2bd, ovl_s, perm, wsplit,
                            nseq=nseq, npages=npages)
        on_s = _s2(pt_i, cache_nsaT, r3(qr_s), selfeat, r3(gates_s), oc_s, r3(nsa_new), r3(win_new), state_t[i],
                   nseq=nseq, npages=npages)
        od_s = _sdiff(pt_i, cache_diff8, r3(dq_s), r3(diff_new), lams, nseq=nseq, npages=npages, lam_init=lam_init)
        p_s = jnp.pad(p_sample[i], ((0, 0), (0, DEC_ROWS - dec), (0, 0))).reshape(nseq * DEC_ROWS, -1)
        xs = _post(xs, on_s.reshape(nseq * DEC_ROWS, 512), od_s.reshape(nseq * DEC_ROWS, 512), p_s, *post_w,
                   lam_init=lam_init)
        win_new4 = r3(win_new)[:, :dec].reshape(nseq, dec, 2, NSA_KV, HEAD_DIM)
        outs[1].append(r3(nsa_new)[:, :dec].reshape(nseq, dec, 4, NSA_KV, HEAD_DIM))
        outs[3].append(r3(diff_new)[:, :dec].reshape(nseq, dec, 2, DIFF_HEADS, DIFF_V))
        win_all = jnp.concatenate([state_nsa_win[i], win_new4], axis=1)
        keep = min(WINDOW, past + dec)
        outs[5].append(win_all[:, win_all.shape[1] - keep:])

    y_p = xp.reshape(b, s, d)
    y_s = xs.reshape(nseq, DEC_ROWS, d)[:, :dec]
    return (y_p, y_s) + tuple(jnp.stack(o) for o in outs)
```

```python
import functools
import math

import numpy as np
import jax
import jax.numpy as jnp
from jax import lax
from jax.experimental import pallas as pl
from jax.experimental.pallas import tpu as pltpu

F32 = jnp.float32
BF16 = jnp.bfloat16

HEAD_DIM = 64
NSA_HEADS = 8
NSA_KV = 2
NSA_REP = NSA_HEADS // NSA_KV
CMP_BLOCK = 32
CMP_STRIDE = 16
CMP_HIDDEN = 128
SEL_BLOCK = 64
SEL_TOPK = 16
WINDOW = 512
DIFF_HEADS = 4
DIFF_QK = 64
DIFF_V = 2 * DIFF_QK
ROPE_THETA = 10000.0
EPS = 1e-6
NEG = -1e30
FORCED = 1e9
Q_SCALE = (HEAD_DIM ** -0.5) * math.log2(math.e)

V7X_LANES = 128
V7X_SUBLANES = 8
V7X_BF16_SUBLANES = 16
V7X_VMEM_BYTES = 64 * 1024 * 1024
V7X_VMEM_USABLE = V7X_VMEM_BYTES - 8 * 1024 * 1024

DEC_ROWS = V7X_SUBLANES
ONES_ROWS = V7X_BF16_SUBLANES
Q_TILE = 128
DQ_TILE = 128
K_TILE = 512
TOK_TILE = 512
SAMPLE_PAGES = 32

_C_Q = 0
_C_KV = 512
_C_DQ = 1280
_C_DK = 1792
_C_DV = 2304
_C_GATE = 2816
_C_END = 2944


def _cparams(sem, vmem_bytes):
    return pltpu.CompilerParams(dimension_semantics=sem,
                                vmem_limit_bytes=int(min(max(vmem_bytes, 16 << 20), V7X_VMEM_USABLE)))


def _dot(a, b):
    return jnp.dot(a, b, preferred_element_type=F32)


def _dot_nt(a, b):
    return lax.dot_general(a, b, (((1,), (1,)), ((), ())), preferred_element_type=F32)


def _rms(x, g):
    return x * lax.rsqrt(jnp.mean(x * x, axis=-1, keepdims=True) + EPS) * g


def _proj_kernel(x_ref, g_ref, w_ref, cos_ref, sin_ref, *outs, tm, nblk_tab, prompt):
    qc_ref, qr_ref, dq_ref, gate_ref, nsa_ref, win_ref, diff_ref = outs[:7]
    x = x_ref[...]
    h = _rms(x, g_ref[...]).astype(BF16)
    cos = cos_ref[...]
    sin = sin_ref[...]
    lane = lax.broadcasted_iota(jnp.int32, (tm, V7X_LANES), 1)
    low = lane < HEAD_DIM
    first = (lane & (HEAD_DIM - 1)) < (HEAD_DIM // 2)

    def rope(zc):
        partner = jnp.where(first, pltpu.roll(zc, 96, 1), pltpu.roll(zc, 32, 1))
        return zc * cos + partner * sin

    def pad_heads(zc):
        return jnp.where(low, zc, 0.0), jnp.where(low, pltpu.roll(zc, 64, 1), 0.0)

    def mm(a, b):
        return _dot(h, w_ref[:, a:b])

    zq = mm(_C_Q, _C_KV)
    for c in range(4):
        zc = zq[:, c * 128:(c + 1) * 128]
        e, o = pad_heads(zc * Q_SCALE)
        qc_ref[:, (2 * c) * 128:(2 * c + 1) * 128] = e.astype(BF16)
        qc_ref[:, (2 * c + 1) * 128:(2 * c + 2) * 128] = o.astype(BF16)
        e, o = pad_heads(rope(zc) * Q_SCALE)
        qr_ref[:, (2 * c) * 128:(2 * c + 1) * 128] = e.astype(BF16)
        qr_ref[:, (2 * c + 1) * 128:(2 * c + 2) * 128] = o.astype(BF16)

    zkv = mm(_C_KV, _C_DQ)
    k2 = rope(zkv[:, 256:384])
    v3 = zkv[:, 384:512]
    k4 = rope(zkv[:, 512:640])
    v5 = zkv[:, 640:768]
    win_ref[:, 0:128] = k4
    win_ref[:, 128:256] = v5

    zdq = mm(_C_DQ, _C_DK)
    for c in range(4):
        dq_ref[:, c * 128:(c + 1) * 128] = (rope(zdq[:, c * 128:(c + 1) * 128]) * Q_SCALE).astype(BF16)
    zdk = mm(_C_DK, _C_DV)
    dk_rot = [rope(zdk[:, c * 128:(c + 1) * 128]) for c in range(4)]
    zdv = mm(_C_DV, _C_GATE)
    gate_ref[...] = jax.nn.sigmoid(mm(_C_GATE, _C_END))

    if not prompt:
        nsa_ref[:, 0:256] = zkv[:, 0:256]
        nsa_ref[:, 256:384] = k2
        nsa_ref[:, 384:512] = v3
        for c in range(4):
            diff_ref[:, c * 128:(c + 1) * 128] = dk_rot[c]
        diff_ref[:, 512:1024] = zdv
        return

    cmp_ref, kaug_ref, vslcT_ref, kwin_ref, vwinT_ref, dk_ref, dvT_ref = outs[7:]
    cmp_ref[...] = zkv[:, 0:256]
    v3t = v3.T
    nsa_ref[0:128, :] = zkv[:, 0:128].T
    nsa_ref[128:256, :] = zkv[:, 128:256].T
    nsa_ref[256:384, :] = k2.T
    nsa_ref[384:512, :] = v3t
    for c in range(4):
        diff_ref[pl.ds(c, tm, stride=8), :] = dk_rot[c]
        diff_ref[pl.ds(4 + c, tm, stride=8), :] = zdv[:, c * 128:(c + 1) * 128]

    base = (pl.program_id(0) % nblk_tab) * tm
    pos = base + lax.broadcasted_iota(jnp.int32, (tm, V7X_LANES), 0)
    onehot = jnp.where(lane == (pos >> 6), 1.0, 0.0).astype(BF16)
    ones = jnp.ones((ONES_ROWS, tm), BF16)
    e, o = pad_heads(k2)
    kaug_ref[:, 0:128] = e.astype(BF16)
    kaug_ref[:, 128:256] = onehot
    kaug_ref[:, 256:384] = o.astype(BF16)
    kaug_ref[:, 384:512] = onehot
    for g in range(NSA_KV):
        vslcT_ref[g, 0:HEAD_DIM, :] = v3t[g * HEAD_DIM:(g + 1) * HEAD_DIM, :].astype(BF16)
        vslcT_ref[g, HEAD_DIM:HEAD_DIM + ONES_ROWS, :] = ones
    e, o = pad_heads(k4)
    kwin_ref[:, 0:128] = e.astype(BF16)
    kwin_ref[:, 128:256] = o.astype(BF16)
    v5t = v5.T.astype(BF16)
    for j in range(tm // 128):
        vwinT_ref[j] = v5t[:, j * 128:(j + 1) * 128]
    for c in range(4):
        dk_ref[:, c * 128:(c + 1) * 128] = dk_rot[c].astype(BF16)
        dvT_ref[c, 0:DIFF_V, :] = zdv[:, c * 128:(c + 1) * 128].T.astype(BF16)
        dvT_ref[c, DIFF_V:DIFF_V + ONES_ROWS, :] = ones


def _project(x2d, g, wcat, cos_tab, sin_tab, *, seq_len, prompt):
    t, d = x2d.shape
    tm = min(TOK_TILE, t)
    assert t % tm == 0 and cos_tab.shape[0] % tm == 0
    nblk_tab = cos_tab.shape[0] // tm
    nblk = t // tm
    row = lambda w: pl.BlockSpec((tm, w), lambda i: (i, 0))
    out_shape = [
        jax.ShapeDtypeStruct((t, 1024), BF16),
        jax.ShapeDtypeStruct((t, 1024), BF16),
        jax.ShapeDtypeStruct((t, 512), BF16),
        jax.ShapeDtypeStruct((t, 128), F32),
    ]
    out_specs = [row(1024), row(1024), row(512), row(128)]
    if not prompt:
        out_shape += [jax.ShapeDtypeStruct((t, 512), F32), jax.ShapeDtypeStruct((t, 256), F32),
                      jax.ShapeDtypeStruct((t, 1024), F32)]
        out_specs += [row(512), row(256), row(1024)]
    else:
        assert seq_len % tm == 0 and tm == K_TILE
        b = t // seq_len
        nt = seq_len // tm
        tile4 = lambda *blk: pl.BlockSpec((None, None) + blk, lambda i: (i // nt, i % nt, 0, 0, 0))
        out_shape += [
            jax.ShapeDtypeStruct((b, 512, seq_len), F32),
            jax.ShapeDtypeStruct((t, 256), F32),
            jax.ShapeDtypeStruct((t * 8, 128), F32),
            jax.ShapeDtypeStruct((t, 256), F32),
            jax.ShapeDtypeStruct((t, 512), BF16),
            jax.ShapeDtypeStruct((b, nt, NSA_KV, HEAD_DIM + ONES_ROWS, tm), BF16),
            jax.ShapeDtypeStruct((t, 256), BF16),
            jax.ShapeDtypeStruct((b, seq_len // 128, 128, 128), BF16),
            jax.ShapeDtypeStruct((t, 512), BF16),
            jax.ShapeDtypeStruct((b, nt, DIFF_HEADS, DIFF_V + ONES_ROWS, tm), BF16),
        ]
        out_specs += [
            pl.BlockSpec((None, 512, tm), lambda i: (i // nt, 0, i % nt)),
            row(256),
            pl.BlockSpec((tm * 8, 128), lambda i: (i, 0)),
            row(256),
            row(512),
            tile4(NSA_KV, HEAD_DIM + ONES_ROWS, tm),
            row(256),
            pl.BlockSpec((None, tm // 128, 128, 128), lambda i: (i // nt, i % nt, 0, 0)),
            row(512),
            tile4(DIFF_HEADS, DIFF_V + ONES_ROWS, tm),
        ]
    return pl.pallas_call(
        functools.partial(_proj_kernel, tm=tm, nblk_tab=nblk_tab, prompt=prompt),
        grid=(nblk,),
        in_specs=[
            row(d),
            pl.BlockSpec((1, d), lambda i: (0, 0)),
            pl.BlockSpec(wcat.shape, lambda i: (0, 0)),
            pl.BlockSpec((tm, 128), lambda i: (i % nblk_tab, 0)),
            pl.BlockSpec((tm, 128), lambda i: (i % nblk_tab, 0)),
        ],
        out_specs=out_specs,
        out_shape=out_shape,
        compiler_params=_cparams(("parallel",), 52 << 20),
        name="proj_prompt" if prompt else "proj_sample",
    )(x2d, g, wcat, cos_tab, sin_tab)


def _pair_lhs(xa, xb, low):
    l0 = jnp.where(low, xa, pltpu.roll(xb, 64, 1))
    l1 = jnp.where(low, pltpu.roll(xa, 64, 1), xb)
    return jnp.concatenate([l0, l1], axis=0).astype(BF16)


def _compress_bias(pos_ref, wpair_ref, s):
    r = jnp.zeros((16, 2 * CMP_HIDDEN), F32)
    for p in range(8):
        lhs = jnp.concatenate([jnp.broadcast_to(pos_ref[0, p:p + 1, :], (8, 128)),
                               jnp.broadcast_to(pos_ref[1, p:p + 1, :], (8, 128))], axis=0).astype(BF16)
        r = r + _dot(lhs, wpair_ref[s, p])
    return r[0:1, 0:CMP_HIDDEN] + r[8:9, CMP_HIDDEN:]


def _compress_finalize(ab_ref, pos_ref, wpair_ref, w2bd_ref, nch):
    row = lax.broadcasted_iota(jnp.int32, (nch, CMP_HIDDEN), 0)
    outs = []
    for s in range(2):
        bias = _compress_bias(pos_ref, wpair_ref, s)
        hid = []
        for g in range(NSA_KV):
            a = ab_ref[s, g * nch:(g + 1) * nch, 0:CMP_HIDDEN]
            bsh = ab_ref[s, pl.ds(g * nch + 1, nch), CMP_HIDDEN:2 * CMP_HIDDEN]
            hg = jax.nn.gelu(a + bsh + bias)
            hid.append(jnp.where(row < nch - 1, hg, 0.0))
        outs.append(_dot(jnp.concatenate(hid, axis=1).astype(BF16), w2bd_ref[s]))
    return outs[0], outs[1]


def _compress_prompt_kernel(rows_k_ref, rows_v_ref, wpair_ref, pos_ref, w2bd_ref, kc2_ref, vcT_ref, ab_ref, *, nch):
    lane = lax.broadcasted_iota(jnp.int32, (nch, V7X_LANES), 1)
    low = lane < HEAD_DIM
    ab_ref[:, 2 * nch:2 * nch + 8, :] = jnp.zeros((2, 8, 2 * CMP_HIDDEN), F32)
    for s, rows_ref in enumerate((rows_k_ref, rows_v_ref)):
        acc = jnp.zeros((2 * nch, 2 * CMP_HIDDEN), F32)
        for p in range(8):
            xa = rows_ref[pl.ds(2 * p, nch, stride=CMP_STRIDE), :]
            xb = rows_ref[pl.ds(2 * p + 1, nch, stride=CMP_STRIDE), :]
            acc = acc + _dot(_pair_lhs(xa, xb, low), wpair_ref[s, p])
        ab_ref[s, 0:2 * nch, :] = acc
    kc, vc = _compress_finalize(ab_ref, pos_ref, wpair_ref, w2bd_ref, nch)
    kc2_ref[0] = kc.astype(BF16)
    kc2_ref[1] = pltpu.roll(kc, 64, 1).astype(BF16)
    vcT_ref[...] = vc.T.astype(BF16)


def _compress_prompt(cmp_rows, wpair, pospair, w2bd, *, b, s):
    nch = s // CMP_STRIDE
    return pl.pallas_call(
        functools.partial(_compress_prompt_kernel, nch=nch),
        grid=(b,),
        in_specs=[
            pl.BlockSpec((s, 128), lambda i: (i, 0)),
            pl.BlockSpec((s, 128), lambda i: (i, 1)),
            pl.BlockSpec(wpair.shape, lambda i: (0, 0, 0, 0)),
            pl.BlockSpec(pospair.shape, lambda i: (0, 0, 0)),
            pl.BlockSpec(w2bd.shape, lambda i: (0, 0, 0)),
        ],
        out_specs=[
            pl.BlockSpec((None, 2, nch, 128), lambda i: (i, 0, 0, 0)),
            pl.BlockSpec((None, 128, nch), lambda i: (i, 0, 0)),
        ],
        out_shape=[jax.ShapeDtypeStruct((b, 2, nch, 128), BF16),
                   jax.ShapeDtypeStruct((b, 128, nch), BF16)],
        scratch_shapes=[pltpu.VMEM((2, 2 * nch + 8, 2 * CMP_HIDDEN), F32)],
        compiler_params=_cparams(("parallel",), 40 << 20),
        name="compress_prompt",
    )(cmp_rows, cmp_rows, wpair, pospair, w2bd)


def _flash_pipeline(nkt, nchain, qk_fn, v_fn, valid_fn, s_refs, p_ref, al_ref, m_ref, acc_ref, tm_ref, sub=128):
    tk = p_ref.shape[1]
    m_ref[...] = jnp.full(m_ref.shape, NEG, F32)
    acc_ref[...] = jnp.zeros(acc_ref.shape, F32)
    p_ref[...] = jnp.zeros(p_ref.shape, BF16)
    al_ref[...] = jnp.ones(al_ref.shape, F32)
    def score(c, kt, dst):
        st = qk_fn(c, kt)
        dst[c] = st
        tm_ref[c] = jnp.max(st, axis=0, keepdims=True)

    for c in range(nchain):
        score(c, 0, s_refs[0])

    def step(i, cur, nxt, masked, do_qk):
        iprev = jnp.maximum(i - 1, 0)
        for c in range(nchain):
            acc_ref[c] = al_ref[c] * acc_ref[c] + _dot(v_fn(c, iprev), p_ref[c])
        tile_max = [tm_ref[c] for c in range(nchain)]
        if do_qk:
            for c in range(nchain):
                score(c, i + 1, nxt)
        for c in range(nchain):
            m_old = m_ref[c]
            if masked:
                m_new = m_old
                for r in range(0, tk, sub):
                    st = jnp.where(valid_fn(i, r, sub), cur[c, r:r + sub, :], NEG)
                    cur[c, r:r + sub, :] = st
                    m_new = jnp.maximum(m_new, jnp.max(st, axis=0, keepdims=True))
            else:
                m_new = jnp.maximum(m_old, tile_max[c])
            al_ref[c] = jnp.exp2(m_old - m_new)
            m_ref[c] = m_new
            for r in range(0, tk, sub):
                p_ref[c, r:r + sub, :] = jnp.exp2(cur[c, r:r + sub, :] - m_new).astype(BF16)

    s0, s1 = s_refs
    npair = (nkt - 1) // 2

    def body(j, carry):
        step(2 * j, s0, s1, False, True)
        step(2 * j + 1, s1, s0, False, True)
        return carry

    lax.fori_loop(0, npair, body, 0)

    @pl.when(nkt - 1 - 2 * npair == 1)
    def _():
        step(nkt - 2, s0, s1, False, True)
        step(nkt - 1, s1, s0, True, False)

    @pl.when(nkt - 1 - 2 * npair == 0)
    def _():
        step(nkt - 1, s0, s1, True, False)

    for c in range(nchain):
        acc_ref[c] = al_ref[c] * acc_ref[c] + _dot(v_fn(c, nkt - 1), p_ref[c])


def _rank_blocks_T(e_ref, rank_ref, jmax):
    nv = 128 // 8
    srow = lax.broadcasted_iota(jnp.int32, (8, 128), 0)
    rank_ref[...] = jnp.zeros((128, 128), F32)
    for c in range(nv):
        @pl.when(8 * c <= jmax)
        def _():
            ec = e_ref[8 * c:8 * c + 8, :]
            rows = [jnp.broadcast_to(ec[r:r + 1, :], (8, 128)) for r in range(8)]

            def count_into(vs):
                for v in vs:
                    ev = e_ref[8 * v:8 * v + 8, :]
                    cnt = jnp.zeros((8, 128), F32)
                    for r in range(8):
                        if v > c:
                            cnt = cnt + jnp.where(rows[r] >= ev, 1.0, 0.0)
                        elif v < c:
                            cnt = cnt + jnp.where(rows[r] > ev, 1.0, 0.0)
                        else:
                            cnt = cnt + jnp.where(srow > r, jnp.where(rows[r] >= ev, 1.0, 0.0),
                                                  jnp.where(rows[r] > ev, 1.0, 0.0))
                    rank_ref[8 * v:8 * v + 8, :] += cnt

            for v0 in range(0, nv, 4):
                vs = range(v0, v0 + 4)
                if v0 <= c:
                    count_into(vs)
                else:
                    pl.when(8 * v0 <= jmax)(functools.partial(count_into, vs))


def _nsa_prompt_kernel(qc_ref, qr_ref, gate_ref, kc2_ref, vcT_ref, ovlT_ref, kaug_ref, vslcT_ref,
                       kwin_ref, vwinT_ref, o_ref, e_ref, rank_ref, m_ref, acc_ref, ow_ref, s0_ref, s1_ref, p_ref, al_ref,
                       tm_ref, *, tq, tk, nch):
    qb = pl.program_id(1)
    t0 = qb * tq
    ncol = NSA_REP * tq
    hcol = ncol // 2
    col = lax.broadcasted_iota(jnp.int32, (1, ncol), 1)
    qcol = col & (tq - 1)
    tcol = t0 + qcol
    cidx = lax.broadcasted_iota(jnp.int32, (nch, ncol), 0)
    cvalid = (cidx * CMP_STRIDE + (CMP_BLOCK - 1)) <= tcol
    jrow = lax.broadcasted_iota(jnp.int32, (128, tq), 0)
    cur = (t0 + lax.broadcasted_iota(jnp.int32, (128, tq), 1)) >> 6
    jmax = (t0 + tq - 1) >> 6
    nkt = (t0 + tq - 1) // tk + 1
    nwin = WINDOW // tq
    gT = gate_ref[...].T

    ocT = []
    lhs = {}
    sc_raw = [_dot_nt(kc2_ref[g], jnp.concatenate(
        [qc_ref[:, h * 128:(h + 1) * 128] for h in range(g * NSA_REP, (g + 1) * NSA_REP)], axis=0))
        for g in range(NSA_KV)]
    for g in range(NSA_KV):
        heads = [g * NSA_REP + r for r in range(NSA_REP)]
        sc = jnp.where(cvalid, sc_raw[g], NEG)
        ec = jnp.exp2(sc - jnp.max(sc, axis=0, keepdims=True))
        lc = jnp.sum(ec, axis=0, keepdims=True)
        p = ec * jnp.where(tcol >= CMP_BLOCK - 1, 1.0 / lc, 0.0)
        ocT.append(_dot(vcT_ref[g * HEAD_DIM:(g + 1) * HEAD_DIM, :], p.astype(BF16)))
        ps = p[:, 0:tq] + p[:, tq:2 * tq] + p[:, 2 * tq:3 * tq] + p[:, 3 * tq:4 * tq]
        hi = ps.astype(BF16)
        lo = (ps - hi.astype(F32)).astype(BF16)
        impT = _dot(ovlT_ref[...], hi) + _dot(ovlT_ref[...], lo)
        forced = (jrow == 0) | (jrow == cur) | (jrow == cur - 1)
        e_ref[...] = jnp.where(jrow > cur, -1.0, jnp.where(forced, FORCED, impT))
        _rank_blocks_T(e_ref, rank_ref, jmax)
        selT = jnp.where(jrow <= cur, jnp.where(rank_ref[...] < float(SEL_TOPK), 0.0, NEG), NEG)
        self_g = selT.T.astype(BF16)
        lhs[g] = jnp.concatenate(
            [jnp.concatenate([qr_ref[:, h * 128:(h + 1) * 128], self_g], axis=1) for h in heads], axis=0)

    def sel_qk(g, kt):
        k0 = pl.multiple_of(kt * tk, tk)
        return _dot_nt(kaug_ref[pl.ds(k0, tk), g * 256:(g + 1) * 256], lhs[g])

    def sel_valid(kt, r, n):
        return (kt * tk + r + lax.broadcasted_iota(jnp.int32, (n, ncol), 0)) <= tcol

    _flash_pipeline(nkt, NSA_KV, sel_qk, lambda g, kt: vslcT_ref[kt, g], sel_valid,
                    (s0_ref, s1_ref), p_ref, al_ref, m_ref, acc_ref, tm_ref)

    @pl.when(qb >= nwin)
    def _():
        ii = lax.broadcasted_iota(jnp.int32, (tq, ncol), 0)
        chunks = {}
        for g in range(NSA_KV):
            qr_g = jnp.concatenate([qr_ref[:, h * 128:(h + 1) * 128]
                                    for h in range(g * NSA_REP, (g + 1) * NSA_REP)], axis=0)
            for c in range(nwin + 1):
                kst = pl.multiple_of(t0 - WINDOW + c * tq, tq)
                chunks[g, c] = _dot_nt(kwin_ref[pl.ds(kst, tq), g * 128:(g + 1) * 128], qr_g)
        ews = {}
        lws = []
        for g in range(NSA_KV):
            chunks[g, 0] = jnp.where(ii > qcol, chunks[g, 0], NEG)
            chunks[g, nwin] = jnp.where(ii <= qcol, chunks[g, nwin], NEG)
            m = jnp.max(functools.reduce(jnp.maximum, [chunks[g, c] for c in range(nwin + 1)]), axis=0, keepdims=True)
            lw = jnp.zeros((1, ncol), F32)
            for c in range(nwin + 1):
                ew = jnp.exp2(chunks[g, c] - m)
                lw = lw + jnp.sum(ew, axis=0, keepdims=True)
                ews[g, c] = ew.astype(BF16)
            lws.append(lw)
        for g in range(NSA_KV):
            acc = jnp.zeros((HEAD_DIM, ncol), F32)
            for c in range(nwin + 1):
                acc = acc + _dot(vwinT_ref[qb - nwin + c, g * HEAD_DIM:(g + 1) * HEAD_DIM, :], ews[g, c])
            ow_ref[g] = acc * (1.0 / lws[g])

    @pl.when(qb < nwin)
    def _():
        nwk = WINDOW + tq
        for g in range(NSA_KV):
            qr_g = jnp.concatenate([qr_ref[:, h * 128:(h + 1) * 128]
                                    for h in range(g * NSA_REP, (g + 1) * NSA_REP)], axis=0)
            sw = _dot_nt(kwin_ref[0:nwk, g * 128:(g + 1) * 128], qr_g)
            dist = tcol - lax.broadcasted_iota(jnp.int32, (nwk, ncol), 0)
            sw = jnp.where(dist >= 0, sw, NEG)
            ew = jnp.exp2(sw - jnp.max(sw, axis=0, keepdims=True))
            lw = jnp.sum(ew, axis=0, keepdims=True)
            ewb = ew.astype(BF16)
            acc = jnp.zeros((HEAD_DIM, ncol), F32)
            for i in range(nwk // 128):
                acc = acc + _dot(vwinT_ref[i, g * HEAD_DIM:(g + 1) * HEAD_DIM, :], ewb[i * 128:(i + 1) * 128, :])
            ow_ref[g] = acc * (1.0 / lw)

    for g in range(NSA_KV):
        owT = ow_ref[g]
        acc = acc_ref[g]
        osT_g = acc[0:HEAD_DIM] * (1.0 / acc[HEAD_DIM:HEAD_DIM + 1])
        for half in range(2):
            osT = osT_g[:, half * hcol:(half + 1) * hcol]
            parts = []
            for k, r in enumerate((2 * half, 2 * half + 1)):
                h = g * NSA_REP + r
                parts.append(gT[3 * h:3 * h + 1, :] * ocT[g][:, r * tq:(r + 1) * tq]
                             + gT[3 * h + 1:3 * h + 2, :] * osT[:, k * tq:(k + 1) * tq]
                             + gT[3 * h + 2:3 * h + 3, :] * owT[:, r * tq:(r + 1) * tq])
            ch = g * (NSA_REP // 2) + half
            o_ref[:, ch * 128:(ch + 1) * 128] = jnp.concatenate(parts, axis=0).T


def _nsa_prompt(qc, qr, gates, kc2, vcT, ovlT, kaug, vslcT, kwin, vwinT, *, b, s):
    tq, tk = Q_TILE, K_TILE
    nqb = s // tq
    nch = s // CMP_STRIDE
    assert s >= WINDOW + tq and s // SEL_BLOCK <= 128 and s % tk == 0 and tk % tq == 0
    qspec = lambda w: pl.BlockSpec((tq, w), lambda i, j: (i * nqb + j, 0))
    vrows = HEAD_DIM + ONES_ROWS
    ncol = NSA_REP * tq
    return pl.pallas_call(
        functools.partial(_nsa_prompt_kernel, tq=tq, tk=tk, nch=nch),
        grid=(b, nqb),
        in_specs=[
            qspec(1024), qspec(1024), qspec(128),
            pl.BlockSpec((None, 2, nch, 128), lambda i, j: (i, 0, 0, 0)),
            pl.BlockSpec((None, 128, nch), lambda i, j: (i, 0, 0)),
            pl.BlockSpec(ovlT.shape, lambda i, j: (0, 0)),
            pl.BlockSpec((s, 512), lambda i, j: (i, 0)),
            pl.BlockSpec((None, s // tk, NSA_KV, vrows, tk), lambda i, j: (i, 0, 0, 0, 0)),
            pl.BlockSpec((s, 256), lambda i, j: (i, 0)),
            pl.BlockSpec((None, s // 128, 128, 128), lambda i, j: (i, 0, 0, 0)),
        ],
        out_specs=qspec(512),
        out_shape=jax.ShapeDtypeStruct((b * s, 512), F32),
        scratch_shapes=[pltpu.VMEM((128, 128), F32), pltpu.VMEM((128, 128), F32),
                        pltpu.VMEM((NSA_KV, 1, ncol), F32),
                        pltpu.VMEM((NSA_KV, vrows, ncol), F32),
                        pltpu.VMEM((NSA_KV, HEAD_DIM, ncol), F32),
                        pltpu.VMEM((NSA_KV, tk, ncol), F32), pltpu.VMEM((NSA_KV, tk, ncol), F32),
                        pltpu.VMEM((NSA_KV, tk, ncol), BF16), pltpu.VMEM((NSA_KV, 1, ncol), F32),
                        pltpu.VMEM((NSA_KV, 1, ncol), F32)],
        compiler_params=_cparams(("parallel", "arbitrary"), 54 << 20),
        name="nsa_prompt",
    )(qc, qr, gates, kc2, vcT, ovlT, kaug, vslcT, kwin, vwinT)


def _lambda(lq1_ref, lk1_ref, lq2_ref, lk2_ref, lam_init):
    a = jnp.sum(lq1_ref[...] * lk1_ref[...], axis=-1, keepdims=True)
    b = jnp.sum(lq2_ref[...] * lk2_ref[...], axis=-1, keepdims=True)
    return jnp.exp(a) - jnp.exp(b) + lam_init


def _diff_prompt_kernel(dq_ref, dk_ref, dvT_ref, lq1_ref, lk1_ref, lq2_ref, lk2_ref, o_ref, m_ref, acc_ref,
                        s0_ref, s1_ref, p_ref, al_ref, tm_ref, *, tq, tk, lam_init):
    qb = pl.program_id(1)
    t0 = qb * tq
    ncol = 2 * tq
    tcol = t0 + (lax.broadcasted_iota(jnp.int32, (1, ncol), 1) & (tq - 1))
    nkt = (t0 + tq - 1) // tk + 1
    low = lax.broadcasted_iota(jnp.int32, (tq, V7X_LANES), 1) < DIFF_QK
    zero = jnp.zeros((tq, V7X_LANES), BF16)
    qbd = []
    for h in range(DIFF_HEADS):
        dqh = dq_ref[:, h * 128:(h + 1) * 128]
        qbd.append(jnp.concatenate([jnp.where(low, dqh, zero), jnp.where(low, zero, dqh)], axis=0))

    def qk(h, kt):
        k0 = pl.multiple_of(kt * tk, tk)
        return _dot_nt(dk_ref[pl.ds(k0, tk), h * 128:(h + 1) * 128], qbd[h])

    def valid(kt, r, n):
        return (kt * tk + r + lax.broadcasted_iota(jnp.int32, (n, ncol), 0)) <= tcol

    _flash_pipeline(nkt, DIFF_HEADS, qk, lambda h, kt: dvT_ref[kt, h], valid,
                    (s0_ref, s1_ref), p_ref, al_ref, m_ref, acc_ref, tm_ref)
    lam = _lambda(lq1_ref, lk1_ref, lq2_ref, lk2_ref, lam_init)
    for h in range(DIFF_HEADS):
        acc = acc_ref[h]
        o = acc[0:DIFF_V] * (1.0 / acc[DIFF_V:DIFF_V + 1])
        o_ref[:, h * DIFF_V:(h + 1) * DIFF_V] = (o[:, 0:tq] - lam * o[:, tq:2 * tq]).T


def _diff_prompt(dq, dk, dvT, lams, *, b, s, lam_init):
    tq, tk = DQ_TILE, K_TILE
    assert s % tq == 0 and tk % tq == 0
    nqb = s // tq
    vrows = DIFF_V + ONES_ROWS
    ncol = 2 * tq
    lspec = pl.BlockSpec((1, DIFF_QK), lambda i, j: (0, 0))
    return pl.pallas_call(
        functools.partial(_diff_prompt_kernel, tq=tq, tk=tk, lam_init=lam_init),
        grid=(b, nqb),
        in_specs=[
            pl.BlockSpec((tq, 512), lambda i, j: (i * nqb + j, 0)),
            pl.BlockSpec((s, 512), lambda i, j: (i, 0)),
            pl.BlockSpec((None, s // tk, DIFF_HEADS, vrows, tk), lambda i, j: (i, 0, 0, 0, 0)),
            lspec, lspec, lspec, lspec,
        ],
        out_specs=pl.BlockSpec((tq, 512), lambda i, j: (i * nqb + j, 0)),
        out_shape=jax.ShapeDtypeStruct((b * s, 512), F32),
        scratch_shapes=[pltpu.VMEM((DIFF_HEADS, 1, ncol), F32), pltpu.VMEM((DIFF_HEADS, vrows, ncol), F32),
                        pltpu.VMEM((DIFF_HEADS, tk, ncol), F32), pltpu.VMEM((DIFF_HEADS, tk, ncol), F32),
                        pltpu.VMEM((DIFF_HEADS, tk, ncol), BF16), pltpu.VMEM((DIFF_HEADS, 1, ncol), F32),
                        pltpu.VMEM((DIFF_HEADS, 1, ncol), F32)],
        compiler_params=_cparams(("parallel", "arbitrary"), 54 << 20),
        name="diff_prompt",
    )(dq, dk, dvT, *lams)


def _page_specs(pp, rows, row_block):
    return [pl.BlockSpec((None, rows, 128), functools.partial(
        lambda b, c, pt, i: (pt[b, c * pp + i], row_block, 0), i=i)) for i in range(pp)]


def _s1_kernel(pt_ref, *refs, pp, nchk, nj):
    pages = refs[:pp]
    (qc_ref, wpair_ref, pos_ref, w2bd_ref, ovl_ref, perm_ref, wsplit_ref,
     oc_ref, self_ref, ab_ref, y_ref) = refs[pp:]
    c = pl.program_id(1)
    mrows = pp * 8

    @pl.when(c == 0)
    def _():
        ab_ref[:, 2 * nchk:2 * nchk + 8, :] = jnp.zeros((2, 8, 2 * CMP_HIDDEN), F32)

    for i, pg in enumerate(pages):
        y = _dot_nt(perm_ref[...], pg[...].astype(BF16))
        for t in range(CMP_STRIDE):
            y_ref[t, i * 8:(i + 1) * 8, :] = y[t * 8:(t + 1) * 8, :]
    r0 = pl.multiple_of(c * mrows, 8)
    for s in range(2):
        acc = jnp.zeros((mrows, 4 * CMP_HIDDEN), F32)
        for p in range(CMP_STRIDE // 2):
            x = jnp.concatenate([y_ref[2 * p, :, s * 128:(s + 1) * 128],
                                 y_ref[2 * p + 1, :, s * 128:(s + 1) * 128]], axis=1).astype(BF16)
            acc = acc + _dot(x, wsplit_ref[s, p])
        ab_ref[s, pl.ds(r0, mrows), :] = acc[:, 0:2 * CMP_HIDDEN]
        ab_ref[s, pl.ds(nchk + r0, mrows), :] = acc[:, 2 * CMP_HIDDEN:4 * CMP_HIDDEN]

    @pl.when(c == pl.num_programs(1) - 1)
    def _():
        kc, vc = _compress_finalize(ab_ref, pos_ref, wpair_ref, w2bd_ref, nchk)
        kcb = kc.astype(BF16)
        vcb = vc.astype(BF16)
        nrow = NSA_REP * DEC_ROWS
        lane = lax.broadcasted_iota(jnp.int32, (DEC_ROWS, V7X_LANES), 1)
        low = lane < HEAD_DIM
        past = nchk * CMP_STRIDE
        trow = past + (lax.broadcasted_iota(jnp.int32, (nrow, nchk), 0) & (DEC_ROWS - 1))
        cvalid = (lax.broadcasted_iota(jnp.int32, (nrow, nchk), 1) * CMP_STRIDE + (CMP_BLOCK - 1)) <= trow
        for g in range(NSA_KV):
            heads = [g * NSA_REP + r for r in range(NSA_REP)]
            q = jnp.concatenate([qc_ref[:, h * 128:(h + 1) * 128] for h in heads], axis=0).astype(F32)
            if g == 1:
                q = pltpu.roll(q, 64, 1)
            s_c = jnp.where(cvalid, _dot_nt(q.astype(BF16), kcb), NEG)
            m = jnp.max(s_c, axis=1, keepdims=True)
            e = jnp.where(cvalid, jnp.exp2(s_c - m), 0.0)
            l = jnp.sum(e, axis=1, keepdims=True)
            p = e * (1.0 / jnp.where(l > 0.0, l, 1.0))
            o = _dot(p.astype(BF16), vcb)
            for mpair in range(NSA_REP // 2):
                ev = o[(2 * mpair) * DEC_ROWS:(2 * mpair + 1) * DEC_ROWS, :]
                od = o[(2 * mpair + 1) * DEC_ROWS:(2 * mpair + 2) * DEC_ROWS, :]
                pair = jnp.where(low, ev, pltpu.roll(od, 64, 1)) if g == 0 else jnp.where(low, pltpu.roll(ev, 64, 1), od)
                ch = g * (NSA_REP // 2) + mpair
                oc_ref[:, ch * 128:(ch + 1) * 128] = pair
            ps = p[0:DEC_ROWS] + p[DEC_ROWS:2 * DEC_ROWS] + p[2 * DEC_ROWS:3 * DEC_ROWS] + p[3 * DEC_ROWS:4 * DEC_ROWS]
            hi = ps.astype(BF16)
            lo = (ps - hi.astype(F32)).astype(BF16)
            imp = _dot(hi, ovl_ref[...]) + _dot(lo, ovl_ref[...])
            ev = jnp.where(lane >= nj, -1.0, jnp.where((lane == 0) | (lane == nj - 1), FORCED, imp))
            rank = jnp.zeros((DEC_ROWS, V7X_LANES), F32)
            for jp in range(nj):
                cb = jnp.broadcast_to(ev[:, jp:jp + 1], (DEC_ROWS, V7X_LANES))
                rank = rank + jnp.where(lane > jp, jnp.where(cb >= ev, 1.0, 0.0), jnp.where(cb > ev, 1.0, 0.0))
            self_ref[g] = jnp.where(lane < nj, jnp.where(rank < float(SEL_TOPK - 1), 0.0, NEG), NEG)


def _s1(page_table, cache_nsaT, qc_s, wpair, pospair, w2bd, ovl, perm, wsplit, *, nseq, npages):
    pp = min(SAMPLE_PAGES, npages)
    assert npages % pp == 0
    nchk = npages * 128 // CMP_STRIDE
    nj = npages * 128 // SEL_BLOCK
    assert nj <= 128
    cmap3 = lambda b, c, pt: (0, 0, 0)
    grid_spec = pltpu.PrefetchScalarGridSpec(
        num_scalar_prefetch=1,
        grid=(nseq, npages // pp),
        in_specs=_page_specs(pp, 256, 0) + [
            pl.BlockSpec((None, DEC_ROWS, 1024), lambda b, c, pt: (b, 0, 0)),
            pl.BlockSpec(wpair.shape, lambda b, c, pt: (0, 0, 0, 0)),
            pl.BlockSpec(pospair.shape, cmap3),
            pl.BlockSpec(w2bd.shape, cmap3),
            pl.BlockSpec(ovl.shape, lambda b, c, pt: (0, 0)),
            pl.BlockSpec(perm.shape, lambda b, c, pt: (0, 0)),
            pl.BlockSpec(wsplit.shape, lambda b, c, pt: (0, 0, 0, 0)),
        ],
        out_specs=[
            pl.BlockSpec((None, DEC_ROWS, 512), lambda b, c, pt: (b, 0, 0)),
            pl.BlockSpec((None, 2, DEC_ROWS, 128), lambda b, c, pt: (b, 0, 0, 0)),
        ],
        scratch_shapes=[pltpu.VMEM((2, 2 * nchk + 8, 2 * CMP_HIDDEN), F32),
                        pltpu.VMEM((CMP_STRIDE, pp * 8, 256), F32)],
    )
    return pl.pallas_call(
        functools.partial(_s1_kernel, pp=pp, nchk=nchk, nj=nj),
        grid_spec=grid_spec,
        out_shape=[jax.ShapeDtypeStruct((nseq, DEC_ROWS, 512), F32),
                   jax.ShapeDtypeStruct((nseq, 2, DEC_ROWS, 128), F32)],
        compiler_params=_cparams(("parallel", "arbitrary"), 40 << 20),
        name="sample_compress_select",
    )(page_table, *([cache_nsaT] * pp), qc_s, wpair, pospair, w2bd, ovl, perm, wsplit)


def _rows_both_groups(q_ref):
    blocks = []
    for g in range(NSA_KV):
        qg = jnp.concatenate([q_ref[:, h * 128:(h + 1) * 128] for h in range(g * NSA_REP, (g + 1) * NSA_REP)], axis=0)
        if g == 1:
            qg = pltpu.roll(qg.astype(F32), 64, 1).astype(BF16)
        blocks.append(qg)
    return jnp.concatenate(blocks, axis=0)


def _s2_kernel(pt_ref, *refs, pp, past):
    pages = refs[:pp]
    (qr_ref, self_ref, gate_ref, oc_ref, nsa_new_ref, win_new_ref, state_ref,
     o_ref, m_ref, l_ref, acc_ref) = refs[pp:]
    c = pl.program_id(1)
    nrow = NSA_KV * NSA_REP * DEC_ROWS
    qall = _rows_both_groups(qr_ref)
    qrow = lax.broadcasted_iota(jnp.int32, (nrow, V7X_LANES), 0) & (DEC_ROWS - 1)
    lane = lax.broadcasted_iota(jnp.int32, (nrow, V7X_LANES), 1)

    @pl.when(c == 0)
    def _():
        m_ref[...] = jnp.full((nrow, 1), NEG, F32)
        l_ref[...] = jnp.zeros((nrow, 1), F32)
        acc_ref[...] = jnp.zeros((nrow, V7X_LANES), F32)

    sel_rows = jnp.concatenate([self_ref[g] for g in range(NSA_KV) for _ in range(NSA_REP)], axis=0).astype(BF16)
    nkeys = pp * 128
    blk = (c * nkeys + lax.broadcasted_iota(jnp.int32, (V7X_LANES, nkeys), 1)) >> 6
    expand = jnp.where(lax.broadcasted_iota(jnp.int32, (V7X_LANES, nkeys), 0) == blk, 1.0, 0.0).astype(BF16)
    s = jnp.concatenate([_dot(qall, pg[0:128, :].astype(BF16)) for pg in pages], axis=1) + _dot(sel_rows, expand)
    m_new = jnp.maximum(m_ref[...], jnp.max(s, axis=1, keepdims=True))
    alpha = jnp.exp2(m_ref[...] - m_new)
    p = jnp.exp2(s - m_new)
    l_ref[...] = alpha * l_ref[...] + jnp.sum(p, axis=1, keepdims=True)
    pv = jnp.zeros((nrow, V7X_LANES), F32)
    for i, pg in enumerate(pages):
        pv = pv + _dot_nt(p[:, i * 128:(i + 1) * 128].astype(BF16), pg[128:256, :].astype(BF16))
    acc_ref[...] = alpha * acc_ref[...] + pv
    m_ref[...] = m_new

    @pl.when(c == pl.num_programs(1) - 1)
    def _():
        zpad = jnp.zeros((V7X_LANES - DEC_ROWS, V7X_LANES), BF16)
        newvalid = lane <= qrow
        kn = jnp.concatenate([nsa_new_ref[:, 256:384].astype(BF16), zpad], axis=0)
        vn = jnp.concatenate([nsa_new_ref[:, 384:512].astype(BF16), zpad], axis=0)
        sn = jnp.where(newvalid, _dot_nt(qall, kn), NEG)
        m2 = jnp.maximum(m_ref[...], jnp.max(sn, axis=1, keepdims=True))
        a2 = jnp.exp2(m_ref[...] - m2)
        pn = jnp.exp2(sn - m2)
        l2 = a2 * l_ref[...] + jnp.sum(pn, axis=1, keepdims=True)
        o_s = (a2 * acc_ref[...] + _dot(pn.astype(BF16), vn)) * (1.0 / l2)
        wb = state_ref.shape[1]
        sw = _dot(qall, state_ref[0:128, :].astype(BF16))
        qrow_w = lax.broadcasted_iota(jnp.int32, (nrow, wb), 0) & (DEC_ROWS - 1)
        dist = (past + qrow_w) - (past - wb + lax.broadcasted_iota(jnp.int32, (nrow, wb), 1))
        sw = jnp.where(dist < WINDOW, sw, NEG)
        kwn = jnp.concatenate([win_new_ref[:, 0:128].astype(BF16), zpad], axis=0)
        vwn = jnp.concatenate([win_new_ref[:, 128:256].astype(BF16), zpad], axis=0)
        swn = jnp.where(newvalid, _dot_nt(qall, kwn), NEG)
        mw = jnp.maximum(jnp.max(sw, axis=1, keepdims=True), jnp.max(swn, axis=1, keepdims=True))
        pw = jnp.exp2(sw - mw)
        pwn = jnp.exp2(swn - mw)
        lw = jnp.sum(pw, axis=1, keepdims=True) + jnp.sum(pwn, axis=1, keepdims=True)
        o_w = (_dot_nt(pw.astype(BF16), state_ref[128:256, :].astype(BF16)) + _dot(pwn.astype(BF16), vwn)) * (1.0 / lw)
        low = lane[0:DEC_ROWS] < HEAD_DIM
        gates = gate_ref[...]

        def pair_rows(x, g, mpair):
            base = g * NSA_REP * DEC_ROWS
            ev = x[base + (2 * mpair) * DEC_ROWS:base + (2 * mpair + 1) * DEC_ROWS, :]
            od = x[base + (2 * mpair + 1) * DEC_ROWS:base + (2 * mpair + 2) * DEC_ROWS, :]
            if g == 0:
                return jnp.where(low, ev, pltpu.roll(od, 64, 1))
            return jnp.where(low, pltpu.roll(ev, 64, 1), od)

        for g in range(NSA_KV):
            for mpair in range(NSA_REP // 2):
                ch = g * (NSA_REP // 2) + mpair
                he, ho = 2 * ch, 2 * ch + 1
                out = jnp.zeros((DEC_ROWS, V7X_LANES), F32)
                branches = (oc_ref[:, ch * 128:(ch + 1) * 128], pair_rows(o_s, g, mpair), pair_rows(o_w, g, mpair))
                for i, br in enumerate(branches):
                    gcol = jnp.where(low, jnp.broadcast_to(gates[:, 3 * he + i:3 * he + i + 1], (DEC_ROWS, V7X_LANES)),
                                     jnp.broadcast_to(gates[:, 3 * ho + i:3 * ho + i + 1], (DEC_ROWS, V7X_LANES)))
                    out = out + gcol * br
                o_ref[:, ch * 128:(ch + 1) * 128] = out


def _s2(page_table, cache_nsaT, qr_s, selfeat, gates_s, oc_s, nsa_new, win_new, state_t, *, nseq, npages):
    pp = min(SAMPLE_PAGES, npages)
    assert npages % pp == 0
    wb = state_t.shape[2]
    past = npages * 128
    assert wb == WINDOW and past >= wb
    nrow = NSA_KV * NSA_REP * DEC_ROWS
    per_seq = lambda *shape: pl.BlockSpec((None,) + shape, lambda b, c, pt: (b,) + (0,) * len(shape))
    grid_spec = pltpu.PrefetchScalarGridSpec(
        num_scalar_prefetch=1,
        grid=(nseq, npages // pp),
        in_specs=_page_specs(pp, 256, 1) + [
            per_seq(DEC_ROWS, 1024), per_seq(2, DEC_ROWS, 128), per_seq(DEC_ROWS, 128), per_seq(DEC_ROWS, 512),
            per_seq(DEC_ROWS, 512), per_seq(DEC_ROWS, 256), per_seq(256, wb),
        ],
        out_specs=per_seq(DEC_ROWS, 512),
        scratch_shapes=[pltpu.VMEM((nrow, 1), F32), pltpu.VMEM((nrow, 1), F32), pltpu.VMEM((nrow, V7X_LANES), F32)],
    )
    return pl.pallas_call(
        functools.partial(_s2_kernel, pp=pp, past=past),
        grid_spec=grid_spec,
        out_shape=jax.ShapeDtypeStruct((nseq, DEC_ROWS, 512), F32),
        compiler_params=_cparams(("parallel", "arbitrary"), 40 << 20),
        name="sample_select_window",
    )(page_table, *([cache_nsaT] * pp), qr_s, selfeat, gates_s, oc_s, nsa_new, win_new, state_t)


def _sdiff_kernel(pt_ref, *refs, pp, lam_init):
    pages = refs[:pp]
    dq_ref, new_ref, lq1_ref, lk1_ref, lq2_ref, lk2_ref, o_ref, m_ref, l_ref, acc_ref = refs[pp:]
    c = pl.program_id(1)
    hrow = 2 * DEC_ROWS
    nrow = DIFF_HEADS * hrow
    lane8 = lax.broadcasted_iota(jnp.int32, (DEC_ROWS, V7X_LANES), 1)
    zero8 = jnp.zeros((DEC_ROWS, V7X_LANES), BF16)
    qh = []
    for h in range(DIFF_HEADS):
        dqh = dq_ref[:, h * 128:(h + 1) * 128]
        qh.append(jnp.concatenate([jnp.where(lane8 < DIFF_QK, dqh, zero8), jnp.where(lane8 < DIFF_QK, zero8, dqh)], axis=0))

    @pl.when(c == 0)
    def _():
        m_ref[...] = jnp.full((nrow, 1), NEG, F32)
        l_ref[...] = jnp.zeros((nrow, 1), F32)
        acc_ref[...] = jnp.zeros((nrow, DIFF_V), F32)

    def keys(pg, h):
        return pg[pl.ds(h, 128, stride=2 * DIFF_HEADS), :].astype(BF16)

    def vals(pg, h):
        return pg[pl.ds(DIFF_HEADS + h, 128, stride=2 * DIFF_HEADS), :].astype(BF16)

    s = jnp.concatenate(
        [jnp.concatenate([_dot_nt(qh[h], keys(pg, h)) for pg in pages], axis=1) for h in range(DIFF_HEADS)], axis=0)
    m_new = jnp.maximum(m_ref[...], jnp.max(s, axis=1, keepdims=True))
    alpha = jnp.exp2(m_ref[...] - m_new)
    p = jnp.exp2(s - m_new)
    l_ref[...] = alpha * l_ref[...] + jnp.sum(p, axis=1, keepdims=True)
    pvs = []
    for h in range(DIFF_HEADS):
        pv = jnp.zeros((hrow, DIFF_V), F32)
        for i, pg in enumerate(pages):
            pv = pv + _dot(p[h * hrow:(h + 1) * hrow, i * 128:(i + 1) * 128].astype(BF16), vals(pg, h))
        pvs.append(pv)
    acc_ref[...] = alpha * acc_ref[...] + jnp.concatenate(pvs, axis=0)
    m_ref[...] = m_new

    @pl.when(c == pl.num_programs(1) - 1)
    def _():
        zpad = jnp.zeros((V7X_LANES - DEC_ROWS, V7X_LANES), BF16)
        qrow = lax.broadcasted_iota(jnp.int32, (hrow, V7X_LANES), 0) & (DEC_ROWS - 1)
        lane = lax.broadcasted_iota(jnp.int32, (hrow, V7X_LANES), 1)
        lam = _lambda(lq1_ref, lk1_ref, lq2_ref, lk2_ref, lam_init)
        for h in range(DIFF_HEADS):
            rows = slice(h * hrow, (h + 1) * hrow)
            kn = jnp.concatenate([new_ref[:, h * 128:(h + 1) * 128].astype(BF16), zpad], axis=0)
            vn = jnp.concatenate([new_ref[:, 512 + h * 128:512 + (h + 1) * 128].astype(BF16), zpad], axis=0)
            sn = jnp.where(lane <= qrow, _dot_nt(qh[h], kn), NEG)
            m1 = m_ref[rows, :]
            m2 = jnp.maximum(m1, jnp.max(sn, axis=1, keepdims=True))
            a2 = jnp.exp2(m1 - m2)
            pn = jnp.exp2(sn - m2)
            l2 = a2 * l_ref[rows, :] + jnp.sum(pn, axis=1, keepdims=True)
            o = (a2 * acc_ref[rows, :] + _dot(pn.astype(BF16), vn)) * (1.0 / l2)
            o_ref[:, h * 128:(h + 1) * 128] = o[0:DEC_ROWS] - lam * o[DEC_ROWS:hrow]


def _sdiff(page_table, cache_diff8, dq_s, diff_new, lams, *, nseq, npages, lam_init):
    pp = min(SAMPLE_PAGES, npages)
    assert npages % pp == 0
    nrow = DIFF_HEADS * 2 * DEC_ROWS
    lspec = pl.BlockSpec((1, DIFF_QK), lambda b, c, pt: (0, 0))
    grid_spec = pltpu.PrefetchScalarGridSpec(
        num_scalar_prefetch=1,
        grid=(nseq, npages // pp),
        in_specs=_page_specs(pp, 1024, 0) + [
            pl.BlockSpec((None, DEC_ROWS, 512), lambda b, c, pt: (b, 0, 0)),
            pl.BlockSpec((None, DEC_ROWS, 1024), lambda b, c, pt: (b, 0, 0)),
            lspec, lspec, lspec, lspec,
        ],
        out_specs=pl.BlockSpec((None, DEC_ROWS, 512), lambda b, c, pt: (b, 0, 0)),
        scratch_shapes=[pltpu.VMEM((nrow, 1), F32), pltpu.VMEM((nrow, 1), F32), pltpu.VMEM((nrow, DIFF_V), F32)],
    )
    return pl.pallas_call(
        functools.partial(_sdiff_kernel, pp=pp, lam_init=lam_init),
        grid_spec=grid_spec,
        out_shape=jax.ShapeDtypeStruct((nseq, DEC_ROWS, 512), F32),
        compiler_params=_cparams(("parallel", "arbitrary"), 52 << 20),
        name="sample_diff",
    )(page_table, *([cache_diff8] * pp), dq_s, diff_new, *lams)


def _post_kernel(x_ref, on_ref, od_ref, p_ref, dg_ref, wo_ref, gpost_ref, gfpre_ref, wup_ref, wdn_ref,
                 gfpost_ref, wple_ref, wgate_ref, gple_ref, y_ref, *, lam_init, ff_chunk):
    x = x_ref[...]
    dg = dg_ref[...] * (1.0 - lam_init)
    parts = [on_ref[...].astype(BF16)]
    for h in range(DIFF_HEADS):
        od = od_ref[:, h * DIFF_V:(h + 1) * DIFF_V]
        od = od * lax.rsqrt(jnp.mean(od * od, axis=-1, keepdims=True) + EPS) * dg
        parts.append(od.astype(BF16))
    cat = jnp.concatenate(parts, axis=1)
    x1 = x + _rms(_dot(cat, wo_ref[...]), gpost_ref[...])
    h1 = _rms(x1, gfpre_ref[...]).astype(BF16)
    f = jnp.zeros(x.shape, F32)
    for c in range(wup_ref.shape[1] // ff_chunk):
        u = jnp.maximum(_dot(h1, wup_ref[:, c * ff_chunk:(c + 1) * ff_chunk]), 0.0)
        f = f + _dot((u * u).astype(BF16), wdn_ref[c * ff_chunk:(c + 1) * ff_chunk, :])
    x2 = x1 + _rms(f, gfpost_ref[...])
    e = _dot(p_ref[...].astype(BF16), wple_ref[...]) * jax.nn.sigmoid(_dot(x2.astype(BF16), wgate_ref[...]))
    y_ref[...] = x2 + _rms(e, gple_ref[...])


def _post(x2d, o_n, o_d, p2d, dgain, wo, gpost, gfpre, wup, wdn, gfpost, wple, wgate, gple, *, lam_init):
    t, d = x2d.shape
    tm = min(TOK_TILE, t)
    assert t % tm == 0
    row = lambda w: pl.BlockSpec((tm, w), lambda i: (i, 0))
    full = lambda a: pl.BlockSpec(a.shape, lambda i: (0,) * a.ndim)
    consts = (dgain, wo, gpost, gfpre, wup, wdn, gfpost, wple, wgate, gple)
    return pl.pallas_call(
        functools.partial(_post_kernel, lam_init=lam_init, ff_chunk=1024),
        grid=(t // tm,),
        in_specs=[row(d), row(512), row(512), row(p2d.shape[1])] + [full(a) for a in consts],
        out_specs=row(d),
        out_shape=jax.ShapeDtypeStruct((t, d), F32),
        compiler_params=_cparams(("parallel",), 54 << 20),
        name="merge_mlp_ple",
    )(x2d, o_n, o_d, p2d, *consts)


def _rope_tables(pos):
    half = HEAD_DIM // 2
    inv = ROPE_THETA ** (-jnp.arange(half, dtype=F32) / half)
    ang = pos.astype(F32)[:, None] * inv[None, :]
    c, s = jnp.cos(ang), jnp.sin(ang)
    return jnp.concatenate([c, c, c, c], axis=-1), jnp.concatenate([-s, s, -s, s], axis=-1)


def _overlap_T(nch):
    cs = np.arange(nch)[None, :] * CMP_STRIDE
    bs = np.arange(128)[:, None] * SEL_BLOCK
    ov = np.clip(np.minimum(cs + CMP_BLOCK, bs + SEL_BLOCK) - np.maximum(cs, bs), 0, None) / CMP_BLOCK
    return ov.astype(np.float32)


def _pair_weights(w1):
    w1r = w1.reshape(CMP_BLOCK, HEAD_DIM, CMP_HIDDEN)
    a = w1r[:CMP_STRIDE].reshape(8, 128, CMP_HIDDEN)
    b = w1r[CMP_STRIDE:].reshape(8, 128, CMP_HIDDEN)
    return jnp.concatenate([a, b], axis=-1)


def _split_weights(w1):
    w1r = w1.reshape(CMP_BLOCK, HEAD_DIM, CMP_HIDDEN)
    ab = jnp.concatenate([w1r[:CMP_STRIDE], w1r[CMP_STRIDE:]], axis=-1)
    z = jnp.zeros_like(ab)
    per_t = jnp.concatenate([jnp.concatenate([ab, z], axis=-1), jnp.concatenate([z, ab], axis=-1)], axis=1)
    return per_t.reshape(CMP_STRIDE // 2, 4 * HEAD_DIM, 4 * CMP_HIDDEN)


def _chunk_perm():
    r = np.arange(128)
    p = np.zeros((128, 128), np.float32)
    p[r, CMP_STRIDE * (r % 8) + r // 8] = 1.0
    return p


def _block_diag2(w2):
    z = jnp.zeros_like(w2)
    return jnp.concatenate([jnp.concatenate([w2, z], axis=1), jnp.concatenate([z, w2], axis=1)], axis=0)


def _token_last(a):
    nd = a.ndim
    t = jnp.transpose(a, (0, 1) + tuple(range(3, nd)) + (2,))
    return t.reshape(a.shape[0], a.shape[1], -1, a.shape[2])


def kernel(x_prompt, x_sample, cache_nsa, cache_diff, state_nsa_win, page_table, p_prompt, p_sample, g_mix_pre, w_in, cmp_pos, cmp_k_w1, cmp_k_w2, cmp_v_w1, cmp_v_w2, diff_lq1, diff_lk1, diff_lq2, diff_lk2, diff_gain, w_out, g_mix_post, g_ffn_pre, w_up, w_down, g_ffn_post, w_ple, w_ple_gate, g_ple):
    b, s, d = x_prompt.shape
    nseq, dec, _ = x_sample.shape
    depth, npool, page = cache_nsa.shape[0], cache_nsa.shape[1], cache_nsa.shape[2]
    npages = page_table.shape[1]
    past = npages * page
    wb = state_nsa_win.shape[2]
    assert page == 128 and dec <= DEC_ROWS and past % SEL_BLOCK == 0 and dec <= SEL_BLOCK
    assert (past + dec - CMP_BLOCK) // CMP_STRIDE + 1 == past // CMP_STRIDE - 1
    assert s % TOK_TILE == 0 and wb == WINDOW and s >= WINDOW

    cos_p, sin_p = _rope_tables(jnp.arange(s))
    pos_s = past + (jnp.arange(nseq * DEC_ROWS) % DEC_ROWS)
    cos_s, sin_s = _rope_tables(pos_s)
    ovlT = jnp.asarray(_overlap_T(s // CMP_STRIDE), BF16)
    ovl_s = jnp.asarray(_overlap_T(past // CMP_STRIDE).T, BF16)
    perm = jnp.asarray(_chunk_perm(), BF16)
    pt = page_table.astype(jnp.int32)
    cache_nsaT = _token_last(cache_nsa).reshape(depth * npool, 4 * NSA_KV * HEAD_DIM, page)
    cache_diff8 = cache_diff.reshape(depth * npool, page * 2 * DIFF_HEADS, DIFF_V)
    state_t = _token_last(state_nsa_win)

    xp = x_prompt.reshape(b * s, d)
    xs = jnp.pad(x_sample, ((0, 0), (0, DEC_ROWS - dec), (0, 0))).reshape(nseq * DEC_ROWS, d)
    outs = [[] for _ in range(6)]
    for i in range(depth):
        lam_init = 0.8 - 0.6 * math.exp(-0.3 * i)
        w = w_in[i]
        wcat = jnp.concatenate([w[:, :1280], w[:, 1304:], w[:, 1280:1304],
                                jnp.zeros((d, _C_END - _C_GATE - 24), w.dtype)], axis=1).astype(BF16)
        wpair = jnp.stack([_pair_weights(cmp_k_w1[i]), _pair_weights(cmp_v_w1[i])]).astype(BF16)
        pospair = jnp.stack([cmp_pos[i][:CMP_STRIDE].reshape(8, 128), cmp_pos[i][CMP_STRIDE:].reshape(8, 128)])
        w2bd = jnp.stack([_block_diag2(cmp_k_w2[i]), _block_diag2(cmp_v_w2[i])]).astype(BF16)
        lams = tuple(a[i].reshape(1, DIFF_QK) for a in (diff_lq1, diff_lk1, diff_lq2, diff_lk2))
        gpre = g_mix_pre[i].reshape(1, d)
        post_w = (diff_gain[i].reshape(1, DIFF_V), w_out[i].astype(BF16), g_mix_post[i].reshape(1, d),
                  g_ffn_pre[i].reshape(1, d), w_up[i].astype(BF16), w_down[i].astype(BF16),
                  g_ffn_post[i].reshape(1, d), w_ple[i].astype(BF16), w_ple_gate[i].astype(BF16),
                  g_ple[i].reshape(1, d))
        pt_i = pt + i * npool

        (qc, qr, dq, gates, nsa_t, win_rows, diff8, cmp_rows,
         kaug, vslcT, kwin, vwinT, dk, dvT) = _project(xp, gpre, wcat, cos_p, sin_p, seq_len=s, prompt=True)
        kc2, vcT = _compress_prompt(cmp_rows, wpair, pospair, w2bd, b=b, s=s)
        o_n = _nsa_prompt(qc, qr, gates, kc2, vcT, ovlT, kaug, vslcT, kwin, vwinT, b=b, s=s)
        o_d = _diff_prompt(dq, dk, dvT, lams, b=b, s=s, lam_init=lam_init)
        xp = _post(xp, o_n, o_d, p_prompt[i].reshape(b * s, -1), *post_w, lam_init=lam_init)
        outs[0].append(jnp.transpose(nsa_t.reshape(b, 4, NSA_KV, HEAD_DIM, s), (0, 4, 1, 2, 3)))
        outs[2].append(diff8.reshape(b, s, 2, DIFF_HEADS, DIFF_V))
        win_tail = lax.optimization_barrier(win_rows.reshape(b, s, 2 * NSA_KV * HEAD_DIM)[:, s - min(WINDOW, s):])
        outs[4].append(win_tail.reshape(b, min(WINDOW, s), 2, NSA_KV, HEAD_DIM))

        qc_s, qr_s, dq_s, gates_s, nsa_new, win_new, diff_new = _project(
            xs, gpre, wcat, cos_s, sin_s, seq_len=None, prompt=False)
        r3 = lambda a: a.reshape(nseq, DEC_ROWS, a.shape[-1])
        wsplit = jnp.stack([_split_weights(cmp_k_w1[i]), _split_weights(cmp_v_w1[i])]).astype(BF16)
        oc_s, selfeat = _s1(pt_i, cache_nsaT, r3(qc_s), wpair, pospair, w2bd, ovl_s, perm, wsplit,
                            nseq=nseq, npages=npages)
        on_s = _s2(pt_i, cache_nsaT, r3(qr_s), selfeat, r3(gates_s), oc_s, r3(nsa_new), r3(win_new), state_t[i],
                   nseq=nseq, npages=npages)
        od_s = _sdiff(pt_i, cache_diff8, r3(dq_s), r3(diff_new), lams, nseq=nseq, npages=npages, lam_init=lam_init)
        p_s = jnp.pad(p_sample[i], ((0, 0), (0, DEC_ROWS - dec), (0, 0))).reshape(nseq * DEC_ROWS, -1)
        xs = _post(xs, on_s.reshape(nseq * DEC_ROWS, 512), od_s.reshape(nseq * DEC_ROWS, 512), p_s, *post_w,
                   lam_init=lam_init)
        win_new4 = r3(win_new)[:, :dec].reshape(nseq, dec, 2, NSA_KV, HEAD_DIM)
        outs[1].append(r3(nsa_new)[:, :dec].reshape(nseq, dec, 4, NSA_KV, HEAD_DIM))
        outs[3].append(r3(diff_new)[:, :dec].reshape(nseq, dec, 2, DIFF_HEADS, DIFF_V))
        win_all = jnp.concatenate([state_nsa_win[i], win_new4], axis=1)
        keep = min(WINDOW, past + dec)
        outs[5].append(win_all[:, win_all.shape[1] - keep:])

    y_p = xp.reshape(b, s, d)
    y_s = xs.reshape(nseq, DEC_ROWS, d)[:, :dec]
    return (y_p, y_s) + tuple(jnp.stack(o) for o in outs)
```

```python
import functools
import math

import numpy as np
import jax
import jax.numpy as jnp
from jax import lax
from jax.experimental import pallas as pl
from jax.experimental.pallas import tpu as pltpu

F32 = jnp.float32
BF16 = jnp.bfloat16

HEAD_DIM = 64
NSA_HEADS = 8
NSA_KV = 2
NSA_REP = NSA_HEADS // NSA_KV
CMP_BLOCK = 32
CMP_STRIDE = 16
CMP_HIDDEN = 128
SEL_BLOCK = 64
SEL_TOPK = 16
WINDOW = 512
DIFF_HEADS = 4
DIFF_QK = 64
DIFF_V = 2 * DIFF_QK
ROPE_THETA = 10000.0
EPS = 1e-6
NEG = -1e30
FORCED = 1e9
Q_SCALE = (HEAD_DIM ** -0.5) * math.log2(math.e)

V7X_LANES = 128
V7X_SUBLANES = 8
V7X_BF16_SUBLANES = 16
V7X_VMEM_BYTES = 64 * 1024 * 1024
V7X_VMEM_USABLE = V7X_VMEM_BYTES - 8 * 1024 * 1024

DEC_ROWS = V7X_SUBLANES
ONES_ROWS = V7X_BF16_SUBLANES
Q_TILE = 128
DQ_TILE = 128
K_TILE = 512
TOK_TILE = 512
SAMPLE_PAGES = 32

_C_Q = 0
_C_KV = 512
_C_DQ = 1280
_C_DK = 1792
_C_DV = 2304
_C_GATE = 2816
_C_END = 2944


def _cparams(sem, vmem_bytes):
    return pltpu.CompilerParams(dimension_semantics=sem,
                                vmem_limit_bytes=int(min(max(vmem_bytes, 16 << 20), V7X_VMEM_USABLE)))


def _dot(a, b):
    return jnp.dot(a, b, preferred_element_type=F32)


def _dot_nt(a, b):
    return lax.dot_general(a, b, (((1,), (1,)), ((), ())), preferred_element_type=F32)


def _rms(x, g):
    return x * lax.rsqrt(jnp.mean(x * x, axis=-1, keepdims=True) + EPS) * g


def _proj_kernel(x_ref, g_ref, w_ref, cos_ref, sin_ref, *outs, tm, nblk_tab, prompt):
    qc_ref, qr_ref, dq_ref, gate_ref, nsa_ref, win_ref, diff_ref = outs[:7]
    x = x_ref[...]
    h = _rms(x, g_ref[...]).astype(BF16)
    cos = cos_ref[...]
    sin = sin_ref[...]
    lane = lax.broadcasted_iota(jnp.int32, (tm, V7X_LANES), 1)
    low = lane < HEAD_DIM
    first = (lane & (HEAD_DIM - 1)) < (HEAD_DIM // 2)

    def rope(zc):
        partner = jnp.where(first, pltpu.roll(zc, 96, 1), pltpu.roll(zc, 32, 1))
        return zc * cos + partner * sin

    def pad_heads(zc):
        return jnp.where(low, zc, 0.0), jnp.where(low, pltpu.roll(zc, 64, 1), 0.0)

    def mm(a, b):
        return _dot(h, w_ref[:, a:b])

    zq = mm(_C_Q, _C_KV)
    for c in range(4):
        zc = zq[:, c * 128:(c + 1) * 128]
        e, o = pad_heads(zc * Q_SCALE)
        qc_ref[:, (2 * c) * 128:(2 * c + 1) * 128] = e.astype(BF16)
        qc_ref[:, (2 * c + 1) * 128:(2 * c + 2) * 128] = o.astype(BF16)
        e, o = pad_heads(rope(zc) * Q_SCALE)
        qr_ref[:, (2 * c) * 128:(2 * c + 1) * 128] = e.astype(BF16)
        qr_ref[:, (2 * c + 1) * 128:(2 * c + 2) * 128] = o.astype(BF16)

    zkv = mm(_C_KV, _C_DQ)
    k2 = rope(zkv[:, 256:384])
    v3 = zkv[:, 384:512]
    k4 = rope(zkv[:, 512:640])
    v5 = zkv[:, 640:768]
    win_ref[:, 0:128] = k4
    win_ref[:, 128:256] = v5

    zdq = mm(_C_DQ, _C_DK)
    for c in range(4):
        dq_ref[:, c * 128:(c + 1) * 128] = (rope(zdq[:, c * 128:(c + 1) * 128]) * Q_SCALE).astype(BF16)
    zdk = mm(_C_DK, _C_DV)
    dk_rot = [rope(zdk[:, c * 128:(c + 1) * 128]) for c in range(4)]
    zdv = mm(_C_DV, _C_GATE)
    gate_ref[...] = jax.nn.sigmoid(mm(_C_GATE, _C_END))

    if not prompt:
        nsa_ref[:, 0:256] = zkv[:, 0:256]
        nsa_ref[:, 256:384] = k2
        nsa_ref[:, 384:512] = v3
        for c in range(4):
            diff_ref[:, c * 128:(c + 1) * 128] = dk_rot[c]
        diff_ref[:, 512:1024] = zdv
        return

    cmp_ref, kaug_ref, vslcT_ref, kwin_ref, vwinT_ref, dk_ref, dvT_ref = outs[7:]
    cmp_ref[...] = zkv[:, 0:256]
    v3t = v3.T
    nsa_ref[0:128, :] = zkv[:, 0:128].T
    nsa_ref[128:256, :] = zkv[:, 128:256].T
    nsa_ref[256:384, :] = k2.T
    nsa_ref[384:512, :] = v3t
    for c in range(4):
        diff_ref[pl.ds(c, tm, stride=8), :] = dk_rot[c]
        diff_ref[pl.ds(4 + c, tm, stride=8), :] = zdv[:, c * 128:(c + 1) * 128]

    base = (pl.program_id(0) % nblk_tab) * tm
    pos = base + lax.broadcasted_iota(jnp.int32, (tm, V7X_LANES), 0)
    onehot = jnp.where(lane == (pos >> 6), 1.0, 0.0).astype(BF16)
    ones = jnp.ones((ONES_ROWS, tm), BF16)
    e, o = pad_heads(k2)
    kaug_ref[:, 0:128] = e.astype(BF16)
    kaug_ref[:, 128:256] = onehot
    kaug_ref[:, 256:384] = o.astype(BF16)
    kaug_ref[:, 384:512] = onehot
    for g in range(NSA_KV):
        vslcT_ref[g, 0:HEAD_DIM, :] = v3t[g * HEAD_DIM:(g + 1) * HEAD_DIM, :].astype(BF16)
        vslcT_ref[g, HEAD_DIM:HEAD_DIM + ONES_ROWS, :] = ones
    e, o = pad_heads(k4)
    kwin_ref[:, 0:128] = e.astype(BF16)
    kwin_ref[:, 128:256] = o.astype(BF16)
    v5t = v5.T.astype(BF16)
    for j in range(tm // 128):
        vwinT_ref[j] = v5t[:, j * 128:(j + 1) * 128]
    for c in range(4):
        dk_ref[:, c * 128:(c + 1) * 128] = dk_rot[c].astype(BF16)
        dvT_ref[c, 0:DIFF_V, :] = zdv[:, c * 128:(c + 1) * 128].T.astype(BF16)
        dvT_ref[c, DIFF_V:DIFF_V + ONES_ROWS, :] = ones


def _project(x2d, g, wcat, cos_tab, sin_tab, *, seq_len, prompt):
    t, d = x2d.shape
    tm = min(TOK_TILE, t)
    assert t % tm == 0 and cos_tab.shape[0] % tm == 0
    nblk_tab = cos_tab.shape[0] // tm
    nblk = t // tm
    row = lambda w: pl.BlockSpec((tm, w), lambda i: (i, 0))
    out_shape = [
        jax.ShapeDtypeStruct((t, 1024), BF16),
        jax.ShapeDtypeStruct((t, 1024), BF16),
        jax.ShapeDtypeStruct((t, 512), BF16),
        jax.ShapeDtypeStruct((t, 128), F32),
    ]
    out_specs = [row(1024), row(1024), row(512), row(128)]
    if not prompt:
        out_shape += [jax.ShapeDtypeStruct((t, 512), F32), jax.ShapeDtypeStruct((t, 256), F32),
                      jax.ShapeDtypeStruct((t, 1024), F32)]
        out_specs += [row(512), row(256), row(1024)]
    else:
        assert seq_len % tm == 0 and tm == K_TILE
        b = t // seq_len
        nt = seq_len // tm
        tile4 = lambda *blk: pl.BlockSpec((None, None) + blk, lambda i: (i // nt, i % nt, 0, 0, 0))
        out_shape += [
            jax.ShapeDtypeStruct((b, 512, seq_len), F32),
            jax.ShapeDtypeStruct((t, 256), F32),
            jax.ShapeDtypeStruct((t * 8, 128), F32),
            jax.ShapeDtypeStruct((t, 256), F32),
            jax.ShapeDtypeStruct((t, 512), BF16),
            jax.ShapeDtypeStruct((b, nt, NSA_KV, HEAD_DIM + ONES_ROWS, tm), BF16),
            jax.ShapeDtypeStruct((t, 256), BF16),
            jax.ShapeDtypeStruct((b, seq_len // 128, 128, 128), BF16),
            jax.ShapeDtypeStruct((t, 512), BF16),
            jax.ShapeDtypeStruct((b, nt, DIFF_HEADS, DIFF_V + ONES_ROWS, tm), BF16),
        ]
        out_specs += [
            pl.BlockSpec((None, 512, tm), lambda i: (i // nt, 0, i % nt)),
            row(256),
            pl.BlockSpec((tm * 8, 128), lambda i: (i, 0)),
            row(256),
            row(512),
            tile4(NSA_KV, HEAD_DIM + ONES_ROWS, tm),
            row(256),
            pl.BlockSpec((None, tm // 128, 128, 128), lambda i: (i // nt, i % nt, 0, 0)),
            row(512),
            tile4(DIFF_HEADS, DIFF_V + ONES_ROWS, tm),
        ]
    return pl.pallas_call(
        functools.partial(_proj_kernel, tm=tm, nblk_tab=nblk_tab, prompt=prompt),
        grid=(nblk,),
        in_specs=[
            row(d),
            pl.BlockSpec((1, d), lambda i: (0, 0)),
            pl.BlockSpec(wcat.shape, lambda i: (0, 0)),
            pl.BlockSpec((tm, 128), lambda i: (i % nblk_tab, 0)),
            pl.BlockSpec((tm, 128), lambda i: (i % nblk_tab, 0)),
        ],
        out_specs=out_specs,
        out_shape=out_shape,
        compiler_params=_cparams(("parallel",), 52 << 20),
        name="proj_prompt" if prompt else "proj_sample",
    )(x2d, g, wcat, cos_tab, sin_tab)


def _pair_lhs(xa, xb, low):
    l0 = jnp.where(low, xa, pltpu.roll(xb, 64, 1))
    l1 = jnp.where(low, pltpu.roll(xa, 64, 1), xb)
    return jnp.concatenate([l0, l1], axis=0).astype(BF16)


def _compress_bias(pos_ref, wpair_ref, s):
    r = jnp.zeros((16, 2 * CMP_HIDDEN), F32)
    for p in range(8):
        lhs = jnp.concatenate([jnp.broadcast_to(pos_ref[0, p:p + 1, :], (8, 128)),
                               jnp.broadcast_to(pos_ref[1, p:p + 1, :], (8, 128))], axis=0).astype(BF16)
        r = r + _dot(lhs, wpair_ref[s, p])
    return r[0:1, 0:CMP_HIDDEN] + r[8:9, CMP_HIDDEN:]


def _compress_finalize(ab_ref, pos_ref, wpair_ref, w2bd_ref, nch):
    row = lax.broadcasted_iota(jnp.int32, (nch, CMP_HIDDEN), 0)
    outs = []
    for s in range(2):
        bias = _compress_bias(pos_ref, wpair_ref, s)
        hid = []
        for g in range(NSA_KV):
            a = ab_ref[s, g * nch:(g + 1) * nch, 0:CMP_HIDDEN]
            bsh = ab_ref[s, pl.ds(g * nch + 1, nch), CMP_HIDDEN:2 * CMP_HIDDEN]
            hg = jax.nn.gelu(a + bsh + bias)
            hid.append(jnp.where(row < nch - 1, hg, 0.0))
        outs.append(_dot(jnp.concatenate(hid, axis=1).astype(BF16), w2bd_ref[s]))
    return outs[0], outs[1]


def _compress_prompt_kernel(rows_k_ref, rows_v_ref, wpair_ref, pos_ref, w2bd_ref, kc2_ref, vcT_ref, ab_ref, *, nch):
    lane = lax.broadcasted_iota(jnp.int32, (nch, V7X_LANES), 1)
    low = lane < HEAD_DIM
    ab_ref[:, 2 * nch:2 * nch + 8, :] = jnp.zeros((2, 8, 2 * CMP_HIDDEN), F32)
    for s, rows_ref in enumerate((rows_k_ref, rows_v_ref)):
        acc = jnp.zeros((2 * nch, 2 * CMP_HIDDEN), F32)
        for p in range(8):
            xa = rows_ref[pl.ds(2 * p, nch, stride=CMP_STRIDE), :]
            xb = rows_ref[pl.ds(2 * p + 1, nch, stride=CMP_STRIDE), :]
            acc = acc + _dot(_pair_lhs(xa, xb, low), wpair_ref[s, p])
        ab_ref[s, 0:2 * nch, :] = acc
    kc, vc = _compress_finalize(ab_ref, pos_ref, wpair_ref, w2bd_ref, nch)
    kc2_ref[0] = kc.astype(BF16)
    kc2_ref[1] = pltpu.roll(kc, 64, 1).astype(BF16)
    vcT_ref[...] = vc.T.astype(BF16)


def _compress_prompt(cmp_rows, wpair, pospair, w2bd, *, b, s):
    nch = s // CMP_STRIDE
    return pl.pallas_call(
        functools.partial(_compress_prompt_kernel, nch=nch),
        grid=(b,),
        in_specs=[
            pl.BlockSpec((s, 128), lambda i: (i, 0)),
            pl.BlockSpec((s, 128), lambda i: (i, 1)),
            pl.BlockSpec(wpair.shape, lambda i: (0, 0, 0, 0)),
            pl.BlockSpec(pospair.shape, lambda i: (0, 0, 0)),
            pl.BlockSpec(w2bd.shape, lambda i: (0, 0, 0)),
        ],
        out_specs=[
            pl.BlockSpec((None, 2, nch, 128), lambda i: (i, 0, 0, 0)),
            pl.BlockSpec((None, 128, nch), lambda i: (i, 0, 0)),
        ],
        out_shape=[jax.ShapeDtypeStruct((b, 2, nch, 128), BF16),
                   jax.ShapeDtypeStruct((b, 128, nch), BF16)],
        scratch_shapes=[pltpu.VMEM((2, 2 * nch + 8, 2 * CMP_HIDDEN), F32)],
        compiler_params=_cparams(("parallel",), 40 << 20),
        name="compress_prompt",
    )(cmp_rows, cmp_rows, wpair, pospair, w2bd)


def _flash_pipeline(nkt, nchain, qk_fn, v_fn, valid_fn, s_refs, p_ref, al_ref, m_ref, acc_ref, tm_ref, sub=128):
    tk = p_ref.shape[1]
    m_ref[...] = jnp.full(m_ref.shape, NEG, F32)
    acc_ref[...] = jnp.zeros(acc_ref.shape, F32)
    p_ref[...] = jnp.zeros(p_ref.shape, BF16)
    al_ref[...] = jnp.ones(al_ref.shape, F32)
    def score(c, kt, dst):
        st = qk_fn(c, kt)
        dst[c] = st
        tm_ref[c] = jnp.max(st, axis=0, keepdims=True)

    for c in range(nchain):
        score(c, 0, s_refs[0])

    def step(i, cur, nxt, masked, do_qk):
        iprev = jnp.maximum(i - 1, 0)
        for c in range(nchain):
            acc_ref[c] = al_ref[c] * acc_ref[c] + _dot(v_fn(c, iprev), p_ref[c])
        tile_max = [tm_ref[c] for c in range(nchain)]
        if do_qk:
            for c in range(nchain):
                score(c, i + 1, nxt)
        for c in range(nchain):
            m_old = m_ref[c]
            if masked:
                m_new = m_old
                for r in range(0, tk, sub):
                    st = jnp.where(valid_fn(i, r, sub), cur[c, r:r + sub, :], NEG)
                    cur[c, r:r + sub, :] = st
                    m_new = jnp.maximum(m_new, jnp.max(st, axis=0, keepdims=True))
            else:
                m_new = jnp.maximum(m_old, tile_max[c])
            al_ref[c] = jnp.exp2(m_old - m_new)
            m_ref[c] = m_new
            for r in range(0, tk, sub):
                p_ref[c, r:r + sub, :] = jnp.exp2(cur[c, r:r + sub, :] - m_new).astype(BF16)

    s0, s1 = s_refs
    npair = (nkt - 1) // 2

    def body(j, carry):
        step(2 * j, s0, s1, False, True)
        step(2 * j + 1, s1, s0, False, True)
        return carry

    lax.fori_loop(0, npair, body, 0)

    @pl.when(nkt - 1 - 2 * npair == 1)
    def _():
        step(nkt - 2, s0, s1, False, True)
        step(nkt - 1, s1, s0, True, False)

    @pl.when(nkt - 1 - 2 * npair == 0)
    def _():
        step(nkt - 1, s0, s1, True, False)

    for c in range(nchain):
        acc_ref[c] = al_ref[c] * acc_ref[c] + _dot(v_fn(c, nkt - 1), p_ref[c])


def _rank_blocks_T(e_ref, rank_ref, jmax):
    nv = 128 // 8
    srow = lax.broadcasted_iota(jnp.int32, (8, 128), 0)
    rank_ref[...] = jnp.zeros((128, 128), F32)
    for c in range(nv):
        @pl.when(8 * c <= jmax)
        def _():
            ec = e_ref[8 * c:8 * c + 8, :]
            rows = [jnp.broadcast_to(ec[r:r + 1, :], (8, 128)) for r in range(8)]

            def count_into(vs):
                for v in vs:
                    ev = e_ref[8 * v:8 * v + 8, :]
                    cnt = jnp.zeros((8, 128), F32)
                    for r in range(8):
                        if v > c:
                            cnt = cnt + jnp.where(rows[r] >= ev, 1.0, 0.0)
                        elif v < c:
                            cnt = cnt + jnp.where(rows[r] > ev, 1.0, 0.0)
                        else:
                            cnt = cnt + jnp.where(srow > r, jnp.where(rows[r] >= ev, 1.0, 0.0),
                                                  jnp.where(rows[r] > ev, 1.0, 0.0))
                    rank_ref[8 * v:8 * v + 8, :] += cnt

            for v0 in range(0, nv, 4):
                vs = range(v0, v0 + 4)
                if v0 <= c:
                    count_into(vs)
                else:
                    pl.when(8 * v0 <= jmax)(functools.partial(count_into, vs))


def _nsa_prompt_kernel(qc_ref, qr_ref, gate_ref, kc2_ref, vcT_ref, ovlT_ref, kaug_ref, vslcT_ref,
                       kwin_ref, vwinT_ref, o_ref, e_ref, rank_ref, m_ref, acc_ref, ow_ref, s0_ref, s1_ref, p_ref, al_ref,
                       tm_ref, *, tq, tk, nch):
    qb = pl.program_id(1)
    t0 = qb * tq
    ncol = NSA_REP * tq
    hcol = ncol // 2
    col = lax.broadcasted_iota(jnp.int32, (1, ncol), 1)
    qcol = col & (tq - 1)
    tcol = t0 + qcol
    cidx = lax.broadcasted_iota(jnp.int32, (nch, ncol), 0)
    cvalid = (cidx * CMP_STRIDE + (CMP_BLOCK - 1)) <= tcol
    jrow = lax.broadcasted_iota(jnp.int32, (128, tq), 0)
    cur = (t0 + lax.broadcasted_iota(jnp.int32, (128, tq), 1)) >> 6
    jmax = (t0 + tq - 1) >> 6
    nkt = (t0 + tq - 1) // tk + 1
    nwin = WINDOW // tq
    gT = gate_ref[...].T

    ocT = []
    lhs = {}
    sc_raw = [_dot_nt(kc2_ref[g], jnp.concatenate(
        [qc_ref[:, h * 128:(h + 1) * 128] for h in range(g * NSA_REP, (g + 1) * NSA_REP)], axis=0))
        for g in range(NSA_KV)]
    for g in range(NSA_KV):
        heads = [g * NSA_REP + r for r in range(NSA_REP)]
        sc = jnp.where(cvalid, sc_raw[g], NEG)
        ec = jnp.exp2(sc - jnp.max(sc, axis=0, keepdims=True))
        lc = jnp.sum(ec, axis=0, keepdims=True)
        p = ec * jnp.where(tcol >= CMP_BLOCK - 1, 1.0 / lc, 0.0)
        ocT.append(_dot(vcT_ref[g * HEAD_DIM:(g + 1) * HEAD_DIM, :], p.astype(BF16)))
        ps = p[:, 0:tq] + p[:, tq:2 * tq] + p[:, 2 * tq:3 * tq] + p[:, 3 * tq:4 * tq]
        hi = ps.astype(BF16)
        lo = (ps - hi.astype(F32)).astype(BF16)
        impT = _dot(ovlT_ref[...], hi) + _dot(ovlT_ref[...], lo)
        forced = (jrow == 0) | (jrow == cur) | (jrow == cur - 1)
        e_ref[...] = jnp.where(jrow > cur, -1.0, jnp.where(forced, FORCED, impT))
        _rank_blocks_T(e_ref, rank_ref, jmax)
        selT = jnp.where(jrow <= cur, jnp.where(rank_ref[...] < float(SEL_TOPK), 0.0, NEG), NEG)
        self_g = selT.T.astype(BF16)
        lhs[g] = jnp.concatenate(
            [jnp.concatenate([qr_ref[:, h * 128:(h + 1) * 128], self_g], axis=1) for h in heads], axis=0)

    def sel_qk(g, kt):
        k0 = pl.multiple_of(kt * tk, tk)
        return _dot_nt(kaug_ref[pl.ds(k0, tk), g * 256:(g + 1) * 256], lhs[g])

    def sel_valid(kt, r, n):
        return (kt * tk + r + lax.broadcasted_iota(jnp.int32, (n, ncol), 0)) <= tcol

    _flash_pipeline(nkt, NSA_KV, sel_qk, lambda g, kt: vslcT_ref[kt, g], sel_valid,
                    (s0_ref, s1_ref), p_ref, al_ref, m_ref, acc_ref, tm_ref)

    @pl.when(qb >= nwin)
    def _():
        ii = lax.broadcasted_iota(jnp.int32, (tq, ncol), 0)
        chunks = {}
        for g in range(NSA_KV):
            qr_g = jnp.concatenate([qr_ref[:, h * 128:(h + 1) * 128]
                                    for h in range(g * NSA_REP, (g + 1) * NSA_REP)], axis=0)
            for c in range(nwin + 1):
                kst = pl.multiple_of(t0 - WINDOW + c * tq, tq)
                chunks[g, c] = _dot_nt(kwin_ref[pl.ds(kst, tq), g * 128:(g + 1) * 128], qr_g)
        ews = {}
        lws = []
        for g in range(NSA_KV):
            chunks[g, 0] = jnp.where(ii > qcol, chunks[g, 0], NEG)
            chunks[g, nwin] = jnp.where(ii <= qcol, chunks[g, nwin], NEG)
            m = jnp.max(functools.reduce(jnp.maximum, [chunks[g, c] for c in range(nwin + 1)]), axis=0, keepdims=True)
            lw = jnp.zeros((1, ncol), F32)
            for c in range(nwin + 1):
                ew = jnp.exp2(chunks[g, c] - m)
                lw = lw + jnp.sum(ew, axis=0, keepdims=True)
                ews[g, c] = ew.astype(BF16)
            lws.append(lw)
        for g in range(NSA_KV):
            acc = jnp.zeros((HEAD_DIM, ncol), F32)
            for c in range(nwin + 1):
                acc = acc + _dot(vwinT_ref[qb - nwin + c, g * HEAD_DIM:(g + 1) * HEAD_DIM, :], ews[g, c])
            ow_ref[g] = acc * (1.0 / lws[g])

    @pl.when(qb < nwin)
    def _():
        nwk = WINDOW + tq
        for g in range(NSA_KV):
            qr_g = jnp.concatenate([qr_ref[:, h * 128:(h + 1) * 128]
                                    for h in range(g * NSA_REP, (g + 1) * NSA_REP)], axis=0)
            sw = _dot_nt(kwin_ref[0:nwk, g * 128:(g + 1) * 128], qr_g)
            dist = tcol - lax.broadcasted_iota(jnp.int32, (nwk, ncol), 0)
            sw = jnp.where(dist >= 0, sw, NEG)
            ew = jnp.exp2(sw - jnp.max(sw, axis=0, keepdims=True))
            lw = jnp.sum(ew, axis=0, keepdims=True)
            ewb = ew.astype(BF16)
            acc = jnp.zeros((HEAD_DIM, ncol), F32)
            for i in range(nwk // 128):
                acc = acc + _dot(vwinT_ref[i, g * HEAD_DIM:(g + 1) * HEAD_DIM, :], ewb[i * 128:(i + 1) * 128, :])
            ow_ref[g] = acc * (1.0 / lw)

    for g in range(NSA_KV):
        owT = ow_ref[g]
        acc = acc_ref[g]
        osT_g = acc[0:HEAD_DIM] * (1.0 / acc[HEAD_DIM:HEAD_DIM + 1])
        for half in range(2):
            osT = osT_g[:, half * hcol:(half + 1) * hcol]
            parts = []
            for k, r in enumerate((2 * half, 2 * half + 1)):
                h = g * NSA_REP + r
                parts.append(gT[3 * h:3 * h + 1, :] * ocT[g][:, r * tq:(r + 1) * tq]
                             + gT[3 * h + 1:3 * h + 2, :] * osT[:, k * tq:(k + 1) * tq]
                             + gT[3 * h + 2:3 * h + 3, :] * owT[:, r * tq:(r + 1) * tq])
            ch = g * (NSA_REP // 2) + half
            o_ref[:, ch * 128:(ch + 1) * 128] = jnp.concatenate(parts, axis=0).T


def _nsa_prompt(qc, qr, gates, kc2, vcT, ovlT, kaug, vslcT, kwin, vwinT, *, b, s):
    tq, tk = Q_TILE, K_TILE
    nqb = s // tq
    nch = s // CMP_STRIDE
    assert s >= WINDOW + tq and s // SEL_BLOCK <= 128 and s % tk == 0 and tk % tq == 0
    qspec = lambda w: pl.BlockSpec((tq, w), lambda i, j: (i * nqb + j, 0))
    vrows = HEAD_DIM + ONES_ROWS
    ncol = NSA_REP * tq
    return pl.pallas_call(
        functools.partial(_nsa_prompt_kernel, tq=tq, tk=tk, nch=nch),
        grid=(b, nqb),
        in_specs=[
            qspec(1024), qspec(1024), qspec(128),
            pl.BlockSpec((None, 2, nch, 128), lambda i, j: (i, 0, 0, 0)),
            pl.BlockSpec((None, 128, nch), lambda i, j: (i, 0, 0)),
            pl.BlockSpec(ovlT.shape, lambda i, j: (0, 0)),
            pl.BlockSpec((s, 512), lambda i, j: (i, 0)),
            pl.BlockSpec((None, s // tk, NSA_KV, vrows, tk), lambda i, j: (i, 0, 0, 0, 0)),
            pl.BlockSpec((s, 256), lambda i, j: (i, 0)),
            pl.BlockSpec((None, s // 128, 128, 128), lambda i, j: (i, 0, 0, 0)),
        ],
        out_specs=qspec(512),
        out_shape=jax.ShapeDtypeStruct((b * s, 512), F32),
        scratch_shapes=[pltpu.VMEM((128, 128), F32), pltpu.VMEM((128, 128), F32),
                        pltpu.VMEM((NSA_KV, 1, ncol), F32),
                        pltpu.VMEM((NSA_KV, vrows, ncol), F32),
                        pltpu.VMEM((NSA_KV, HEAD_DIM, ncol), F32),
                        pltpu.VMEM((NSA_KV, tk, ncol), F32), pltpu.VMEM((NSA_KV, tk, ncol), F32),
                        pltpu.VMEM((NSA_KV, tk, ncol), BF16), pltpu.VMEM((NSA_KV, 1, ncol), F32),
                        pltpu.VMEM((NSA_KV, 1, ncol), F32)],
        compiler_params=_cparams(("parallel", "arbitrary"), 54 << 20),
        name="nsa_prompt",
    )(qc, qr, gates, kc2, vcT, ovlT, kaug, vslcT, kwin, vwinT)


def _lambda(lq1_ref, lk1_ref, lq2_ref, lk2_ref, lam_init):
    a = jnp.sum(lq1_ref[...] * lk1_ref[...], axis=-1, keepdims=True)
    b = jnp.sum(lq2_ref[...] * lk2_ref[...], axis=-1, keepdims=True)
    return jnp.exp(a) - jnp.exp(b) + lam_init


def _diff_prompt_kernel(dq_ref, dk_ref, dvT_ref, lq1_ref, lk1_ref, lq2_ref, lk2_ref, o_ref, m_ref, acc_ref,
                        s0_ref, s1_ref, p_ref, al_ref, tm_ref, *, tq, tk, lam_init):
    qb = pl.program_id(1)
    t0 = qb * tq
    ncol = 2 * tq
    tcol = t0 + (lax.broadcasted_iota(jnp.int32, (1, ncol), 1) & (tq - 1))
    nkt = (t0 + tq - 1) // tk + 1
    low = lax.broadcasted_iota(jnp.int32, (tq, V7X_LANES), 1) < DIFF_QK
    zero = jnp.zeros((tq, V7X_LANES), BF16)
    qbd = []
    for h in range(DIFF_HEADS):
        dqh = dq_ref[:, h * 128:(h + 1) * 128]
        qbd.append(jnp.concatenate([jnp.where(low, dqh, zero), jnp.where(low, zero, dqh)], axis=0))

    def qk(h, kt):
        k0 = pl.multiple_of(kt * tk, tk)
        return _dot_nt(dk_ref[pl.ds(k0, tk), h * 128:(h + 1) * 128], qbd[h])

    def valid(kt, r, n):
        return (kt * tk + r + lax.broadcasted_iota(jnp.int32, (n, ncol), 0)) <= tcol

    _flash_pipeline(nkt, DIFF_HEADS, qk, lambda h, kt: dvT_ref[kt, h], valid,
                    (s0_ref, s1_ref), p_ref, al_ref, m_ref, acc_ref, tm_ref)
    lam = _lambda(lq1_ref, lk1_ref, lq2_ref, lk2_ref, lam_init)
    for h in range(DIFF_HEADS):
        acc = acc_ref[h]
        o = acc[0:DIFF_V] * (1.0 / acc[DIFF_V:DIFF_V + 1])
        o_ref[:, h * DIFF_V:(h + 1) * DIFF_V] = (o[:, 0:tq] - lam * o[:, tq:2 * tq]).T


def _diff_prompt(dq, dk, dvT, lams, *, b, s, lam_init):
    tq, tk = DQ_TILE, K_TILE
    assert s % tq == 0 and tk % tq == 0
    nqb = s // tq
    vrows = DIFF_V + ONES_ROWS
    ncol = 2 * tq
    lspec = pl.BlockSpec((1, DIFF_QK), lambda i, j: (0, 0))
    return pl.pallas_call(
        functools.partial(_diff_prompt_kernel, tq=tq, tk=tk, lam_init=lam_init),
        grid=(b, nqb),
        in_specs=[
            pl.BlockSpec((tq, 512), lambda i, j: (i * nqb + j, 0)),
            pl.BlockSpec((s, 512), lambda i, j: (i, 0)),
            pl.BlockSpec((None, s // tk, DIFF_HEADS, vrows, tk), lambda i, j: (i, 0, 0, 0, 0)),
            lspec, lspec, lspec, lspec,
        ],
        out_specs=pl.BlockSpec((tq, 512), lambda i, j: (i * nqb + j, 0)),
        out_shape=jax.ShapeDtypeStruct((b * s, 512), F32),
        scratch_shapes=[pltpu.VMEM((DIFF_HEADS, 1, ncol), F32), pltpu.VMEM((DIFF_HEADS, vrows, ncol), F32),
                        pltpu.VMEM((DIFF_HEADS, tk, ncol), F32), pltpu.VMEM((DIFF_HEADS, tk, ncol), F32),
                        pltpu.VMEM((DIFF_HEADS, tk, ncol), BF16), pltpu.VMEM((DIFF_HEADS, 1, ncol), F32),
                        pltpu.VMEM((DIFF_HEADS, 1, ncol), F32)],
        compiler_params=_cparams(("parallel", "arbitrary"), 54 << 20),
        name="diff_prompt",
    )(dq, dk, dvT, *lams)


def _page_specs(pp, rows, row_block):
    return [pl.BlockSpec((None, rows, 128), functools.partial(
        lambda b, c, pt, i: (pt[b, c * pp + i], row_block, 0), i=i)) for i in range(pp)]


def _s1_kernel(pt_ref, *refs, pp, nchk, nj):
    pages = refs[:pp]
    (qc_ref, wpair_ref, pos_ref, w2bd_ref, ovl_ref, perm_ref, wsplit_ref,
     oc_ref, self_ref, ab_ref, y_ref) = refs[pp:]
    c = pl.program_id(1)
    mrows = pp * 8

    @pl.when(c == 0)
    def _():
        ab_ref[:, 2 * nchk:2 * nchk + 8, :] = jnp.zeros((2, 8, 2 * CMP_HIDDEN), F32)

    for i, pg in enumerate(pages):
        y = _dot_nt(perm_ref[...], pg[...].astype(BF16))
        for t in range(CMP_STRIDE):
            y_ref[t, i * 8:(i + 1) * 8, :] = y[t * 8:(t + 1) * 8, :]
    r0 = pl.multiple_of(c * mrows, 8)
    for s in range(2):
        acc = jnp.zeros((mrows, 4 * CMP_HIDDEN), F32)
        for p in range(CMP_STRIDE // 2):
            x = jnp.concatenate([y_ref[2 * p, :, s * 128:(s + 1) * 128],
                                 y_ref[2 * p + 1, :, s * 128:(s + 1) * 128]], axis=1).astype(BF16)
            acc = acc + _dot(x, wsplit_ref[s, p])
        ab_ref[s, pl.ds(r0, mrows), :] = acc[:, 0:2 * CMP_HIDDEN]
        ab_ref[s, pl.ds(nchk + r0, mrows), :] = acc[:, 2 * CMP_HIDDEN:4 * CMP_HIDDEN]

    @pl.when(c == pl.num_programs(1) - 1)
    def _():
        kc, vc = _compress_finalize(ab_ref, pos_ref, wpair_ref, w2bd_ref, nchk)
        kcb = kc.astype(BF16)
        vcb = vc.astype(BF16)
        nrow = NSA_REP * DEC_ROWS
        lane = lax.broadcasted_iota(jnp.int32, (DEC_ROWS, V7X_LANES), 1)
        low = lane < HEAD_DIM
        past = nchk * CMP_STRIDE
        trow = past + (lax.broadcasted_iota(jnp.int32, (nrow, nchk), 0) & (DEC_ROWS - 1))
        cvalid = (lax.broadcasted_iota(jnp.int32, (nrow, nchk), 1) * CMP_STRIDE + (CMP_BLOCK - 1)) <= trow
        for g in range(NSA_KV):
            heads = [g * NSA_REP + r for r in range(NSA_REP)]
            q = jnp.concatenate([qc_ref[:, h * 128:(h + 1) * 128] for h in heads], axis=0).astype(F32)
            if g == 1:
                q = pltpu.roll(q, 64, 1)
            s_c = jnp.where(cvalid, _dot_nt(q.astype(BF16), kcb), NEG)
            m = jnp.max(s_c, axis=1, keepdims=True)
            e = jnp.where(cvalid, jnp.exp2(s_c - m), 0.0)
            l = jnp.sum(e, axis=1, keepdims=True)
            p = e * (1.0 / jnp.where(l > 0.0, l, 1.0))
            o = _dot(p.astype(BF16), vcb)
            for mpair in range(NSA_REP // 2):
                ev = o[(2 * mpair) * DEC_ROWS:(2 * mpair + 1) * DEC_ROWS, :]
                od = o[(2 * mpair + 1) * DEC_ROWS:(2 * mpair + 2) * DEC_ROWS, :]
                pair = jnp.where(low, ev, pltpu.roll(od, 64, 1)) if g == 0 else jnp.where(low, pltpu.roll(ev, 64, 1), od)
                ch = g * (NSA_REP // 2) + mpair
                oc_ref[:, ch * 128:(ch + 1) * 128] = pair
            ps = p[0:DEC_ROWS] + p[DEC_ROWS:2 * DEC_ROWS] + p[2 * DEC_ROWS:3 * DEC_ROWS] + p[3 * DEC_ROWS:4 * DEC_ROWS]
            hi = ps.astype(BF16)
            lo = (ps - hi.astype(F32)).astype(BF16)
            imp = _dot(hi, ovl_ref[...]) + _dot(lo, ovl_ref[...])
            ev = jnp.where(lane >= nj, -1.0, jnp.where((lane == 0) | (lane == nj - 1), FORCED, imp))
            rank = jnp.zeros((DEC_ROWS, V7X_LANES), F32)
            for jp in range(nj):
                cb = jnp.broadcast_to(ev[:, jp:jp + 1], (DEC_ROWS, V7X_LANES))
                rank = rank + jnp.where(lane > jp, jnp.where(cb >= ev, 1.0, 0.0), jnp.where(cb > ev, 1.0, 0.0))
            self_ref[g] = jnp.where(lane < nj, jnp.where(rank < float(SEL_TOPK - 1), 0.0, NEG), NEG)


def _s1(page_table, cache_nsaT, qc_s, wpair, pospair, w2bd, ovl, perm, wsplit, *, nseq, npages):
    pp = min(SAMPLE_PAGES, npages)
    assert npages % pp == 0
    nchk = npages * 128 // CMP_STRIDE
    nj = npages * 128 // SEL_BLOCK
    assert nj <= 128
    cmap3 = lambda b, c, pt: (0, 0, 0)
    grid_spec = pltpu.PrefetchScalarGridSpec(
        num_scalar_prefetch=1,
        grid=(nseq, npages // pp),
        in_specs=_page_specs(pp, 256, 0) + [
            pl.BlockSpec((None, DEC_ROWS, 1024), lambda b, c, pt: (b, 0, 0)),
            pl.BlockSpec(wpair.shape, lambda b, c, pt: (0, 0, 0, 0)),
            pl.BlockSpec(pospair.shape, cmap3),
            pl.BlockSpec(w2bd.shape, cmap3),
            pl.BlockSpec(ovl.shape, lambda b, c, pt: (0, 0)),
            pl.BlockSpec(perm.shape, lambda b, c, pt: (0, 0)),
            pl.BlockSpec(wsplit.shape, lambda b, c, pt: (0, 0, 0, 0)),
        ],
        out_specs=[
            pl.BlockSpec((None, DEC_ROWS, 512), lambda b, c, pt: (b, 0, 0)),
            pl.BlockSpec((None, 2, DEC_ROWS, 128), lambda b, c, pt: (b, 0, 0, 0)),
        ],
        scratch_shapes=[pltpu.VMEM((2, 2 * nchk + 8, 2 * CMP_HIDDEN), F32),
                        pltpu.VMEM((CMP_STRIDE, pp * 8, 256), F32)],
    )
    return pl.pallas_call(
        functools.partial(_s1_kernel, pp=pp, nchk=nchk, nj=nj),
        grid_spec=grid_spec,
        out_shape=[jax.ShapeDtypeStruct((nseq, DEC_ROWS, 512), F32),
                   jax.ShapeDtypeStruct((nseq, 2, DEC_ROWS, 128), F32)],
        compiler_params=_cparams(("parallel", "arbitrary"), 40 << 20),
        name="sample_compress_select",
    )(page_table, *([cache_nsaT] * pp), qc_s, wpair, pospair, w2bd, ovl, perm, wsplit)


def _rows_both_groups(q_ref):
    blocks = []
    for g in range(NSA_KV):
        qg = jnp.concatenate([q_ref[:, h * 128:(h + 1) * 128] for h in range(g * NSA_REP, (g + 1) * NSA_REP)], axis=0)
        if g == 1:
            qg = pltpu.roll(qg.astype(F32), 64, 1).astype(BF16)
        blocks.append(qg)
    return jnp.concatenate(blocks, axis=0)


def _s2_kernel(pt_ref, *refs, pp, past):
    pages = refs[:pp]
    (qr_ref, self_ref, gate_ref, oc_ref, nsa_new_ref, win_new_ref, state_ref,
     o_ref, m_ref, l_ref, acc_ref) = refs[pp:]
    c = pl.program_id(1)
    nrow = NSA_KV * NSA_REP * DEC_ROWS
    qall = _rows_both_groups(qr_ref)
    qrow = lax.broadcasted_iota(jnp.int32, (nrow, V7X_LANES), 0) & (DEC_ROWS - 1)
    lane = lax.broadcasted_iota(jnp.int32, (nrow, V7X_LANES), 1)

    @pl.when(c == 0)
    def _():
        m_ref[...] = jnp.full((nrow, 1), NEG, F32)
        l_ref[...] = jnp.zeros((nrow, 1), F32)
        acc_ref[...] = jnp.zeros((nrow, V7X_LANES), F32)

    sel_rows = jnp.concatenate([self_ref[g] for g in range(NSA_KV) for _ in range(NSA_REP)], axis=0).astype(BF16)
    nkeys = pp * 128
    blk = (c * nkeys + lax.broadcasted_iota(jnp.int32, (V7X_LANES, nkeys), 1)) >> 6
    expand = jnp.where(lax.broadcasted_iota(jnp.int32, (V7X_LANES, nkeys), 0) == blk, 1.0, 0.0).astype(BF16)
    s = jnp.concatenate([_dot(qall, pg[0:128, :].astype(BF16)) for pg in pages], axis=1) + _dot(sel_rows, expand)
    m_new = jnp.maximum(m_ref[...], jnp.max(s, axis=1, keepdims=True))
    alpha = jnp.exp2(m_ref[...] - m_new)
    p = jnp.exp2(s - m_new)
    l_ref[...] = alpha * l_ref[...] + jnp.sum(p, axis=1, keepdims=True)
    pv = jnp.zeros((nrow, V7X_LANES), F32)
    for i, pg in enumerate(pages):
        pv = pv + _dot_nt(p[:, i * 128:(i + 1) * 128].astype(BF16), pg[128:256, :].astype(BF16))
    acc_ref[...] = alpha * acc_ref[...] + pv
    m_ref[...] = m_new

    @pl.when(c == pl.num_programs(1) - 1)
    def _():
        zpad = jnp.zeros((V7X_LANES - DEC_ROWS, V7X_LANES), BF16)
        newvalid = lane <= qrow
        kn = jnp.concatenate([nsa_new_ref[:, 256:384].astype(BF16), zpad], axis=0)
        vn = jnp.concatenate([nsa_new_ref[:, 384:512].astype(BF16), zpad], axis=0)
        sn = jnp.where(newvalid, _dot_nt(qall, kn), NEG)
        m2 = jnp.maximum(m_ref[...], jnp.max(sn, axis=1, keepdims=True))
        a2 = jnp.exp2(m_ref[...] - m2)
        pn = jnp.exp2(sn - m2)
        l2 = a2 * l_ref[...] + jnp.sum(pn, axis=1, keepdims=True)
        o_s = (a2 * acc_ref[...] + _dot(pn.astype(BF16), vn)) * (1.0 / l2)
        wb = state_ref.shape[1]
        sw = _dot(qall, state_ref[0:128, :].astype(BF16))
        qrow_w = lax.broadcasted_iota(jnp.int32, (nrow, wb), 0) & (DEC_ROWS - 1)
        dist = (past + qrow_w) - (past - wb + lax.broadcasted_iota(jnp.int32, (nrow, wb), 1))
        sw = jnp.where(dist < WINDOW, sw, NEG)
        kwn = jnp.concatenate([win_new_ref[:, 0:128].astype(BF16), zpad], axis=0)
        vwn = jnp.concatenate([win_new_ref[:, 128:256].astype(BF16), zpad], axis=0)
        swn = jnp.where(newvalid, _dot_nt(qall, kwn), NEG)
        mw = jnp.maximum(jnp.max(sw, axis=1, keepdims=True), jnp.max(swn, axis=1, keepdims=True))
        pw = jnp.exp2(sw - mw)
        pwn = jnp.exp2(swn - mw)
        lw = jnp.sum(pw, axis=1, keepdims=True) + jnp.sum(pwn, axis=1, keepdims=True)
        o_w = (_dot_nt(pw.astype(BF16), state_ref[128:256, :].astype(BF16)) + _dot(pwn.astype(BF16), vwn)) * (1.0 / lw)
        low = lane[0:DEC_ROWS] < HEAD_DIM
        gates = gate_ref[...]

        def pair_rows(x, g, mpair):
            base = g * NSA_REP * DEC_ROWS
            ev = x[base + (2 * mpair) * DEC_ROWS:base + (2 * mpair + 1) * DEC_ROWS, :]
            od = x[base + (2 * mpair + 1) * DEC_ROWS:base + (2 * mpair + 2) * DEC_ROWS, :]
            if g == 0:
                return jnp.where(low, ev, pltpu.roll(od, 64, 1))
            return jnp.where(low, pltpu.roll(ev, 64, 1), od)

        for g in range(NSA_KV):
            for mpair in range(NSA_REP // 2):
                ch = g * (NSA_REP // 2) + mpair
                he, ho = 2 * ch, 2 * ch + 1
                out = jnp.zeros((DEC_ROWS, V7X_LANES), F32)
                branches = (oc_ref[:, ch * 128:(ch + 1) * 128], pair_rows(o_s, g, mpair), pair_rows(o_w, g, mpair))
                for i, br in enumerate(branches):
                    gcol = jnp.where(low, jnp.broadcast_to(gates[:, 3 * he + i:3 * he + i + 1], (DEC_ROWS, V7X_LANES)),
                                     jnp.broadcast_to(gates[:, 3 * ho + i:3 * ho + i + 1], (DEC_ROWS, V7X_LANES)))
                    out = out + gcol * br
                o_ref[:, ch * 128:(ch + 1) * 128] = out


def _sdiff_kernel(pt_ref, *refs, pp, lam_init):
    pages = refs[:pp]
    dq_ref, new_ref, lq1_ref, lk1_ref, lq2_ref, lk2_ref, o_ref, m_ref, l_ref, acc_ref = refs[pp:]
    c = pl.program_id(1)
    hrow = 2 * DEC_ROWS
    nrow = DIFF_HEADS * hrow
    lane8 = lax.broadcasted_iota(jnp.int32, (DEC_ROWS, V7X_LANES), 1)
    zero8 = jnp.zeros((DEC_ROWS, V7X_LANES), BF16)
    qh = []
    for h in range(DIFF_HEADS):
        dqh = dq_ref[:, h * 128:(h + 1) * 128]
        qh.append(jnp.concatenate([jnp.where(lane8 < DIFF_QK, dqh, zero8), jnp.where(lane8 < DIFF_QK, zero8, dqh)], axis=0))

    @pl.when(c == 0)
    def _():
        m_ref[...] = jnp.full((nrow, 1), NEG, F32)
        l_ref[...] = jnp.zeros((nrow, 1), F32)
        acc_ref[...] = jnp.zeros((nrow, DIFF_V), F32)

    def keys(pg, h):
        return pg[pl.ds(h, 128, stride=2 * DIFF_HEADS), :].astype(BF16)

    def vals(pg, h):
        return pg[pl.ds(DIFF_HEADS + h, 128, stride=2 * DIFF_HEADS), :].astype(BF16)

    s = jnp.concatenate(
        [jnp.concatenate([_dot_nt(qh[h], keys(pg, h)) for pg in pages], axis=1) for h in range(DIFF_HEADS)], axis=0)
    m_new = jnp.maximum(m_ref[...], jnp.max(s, axis=1, keepdims=True))
    alpha = jnp.exp2(m_ref[...] - m_new)
    p = jnp.exp2(s - m_new)
    l_ref[...] = alpha * l_ref[...] + jnp.sum(p, axis=1, keepdims=True)
    pvs = []
    for h in range(DIFF_HEADS):
        pv = jnp.zeros((hrow, DIFF_V), F32)
        for i, pg in enumerate(pages):
            pv = pv + _dot(p[h * hrow:(h + 1) * hrow, i * 128:(i + 1) * 128].astype(BF16), vals(pg, h))
        pvs.append(pv)
    acc_ref[...] = alpha * acc_ref[...] + jnp.concatenate(pvs, axis=0)
    m_ref[...] = m_new

    @pl.when(c == pl.num_programs(1) - 1)
    def _():
        zpad = jnp.zeros((V7X_LANES - DEC_ROWS, V7X_LANES), BF16)
        qrow = lax.broadcasted_iota(jnp.int32, (hrow, V7X_LANES), 0) & (DEC_ROWS - 1)
        lane = lax.broadcasted_iota(jnp.int32, (hrow, V7X_LANES), 1)
        lam = _lambda(lq1_ref, lk1_ref, lq2_ref, lk2_ref, lam_init)
        for h in range(DIFF_HEADS):
            rows = slice(h * hrow, (h + 1) * hrow)
            kn = jnp.concatenate([new_ref[:, h * 128:(h + 1) * 128].astype(BF16), zpad], axis=0)
            vn = jnp.concatenate([new_ref[:, 512 + h * 128:512 + (h + 1) * 128].astype(BF16), zpad], axis=0)
            sn = jnp.where(lane <= qrow, _dot_nt(qh[h], kn), NEG)
            m1 = m_ref[rows, :]
            m2 = jnp.maximum(m1, jnp.max(sn, axis=1, keepdims=True))
            a2 = jnp.exp2(m1 - m2)
            pn = jnp.exp2(sn - m2)
            l2 = a2 * l_ref[rows, :] + jnp.sum(pn, axis=1, keepdims=True)
            o = (a2 * acc_ref[rows, :] + _dot(pn.astype(BF16), vn)) * (1.0 / l2)
            o_ref[:, h * 128:(h + 1) * 128] = o[0:DEC_ROWS] - lam * o[DEC_ROWS:hrow]


def _s2_sdiff_kernel(pt_ref, *refs, pp, past, lam_init):
    n2, nd = pp + 7, pp + 6
    in2, ind = refs[:n2], refs[n2:n2 + nd]
    o2, od = refs[n2 + nd:n2 + nd + 2]
    scr = refs[n2 + nd + 2:]
    _s2_kernel(pt_ref, *in2, o2, *scr[:3], pp=pp, past=past)
    _sdiff_kernel(pt_ref, *ind, od, *scr[3:], pp=pp, lam_init=lam_init)


def _s2_sdiff(page_table, cache_nsaT, qr_s, selfeat, gates_s, oc_s, nsa_new, win_new, state_t,
              cache_diff8, dq_s, diff_new, lams, *, nseq, npages, lam_init):
    pp = min(SAMPLE_PAGES, npages)
    assert npages % pp == 0
    wb = state_t.shape[2]
    past = npages * 128
    assert wb == WINDOW and past >= wb
    nrow2 = NSA_KV * NSA_REP * DEC_ROWS
    nrowd = DIFF_HEADS * 2 * DEC_ROWS
    per_seq = lambda *shape: pl.BlockSpec((None,) + shape, lambda b, c, pt: (b,) + (0,) * len(shape))
    lspec = pl.BlockSpec((1, DIFF_QK), lambda b, c, pt: (0, 0))
    grid_spec = pltpu.PrefetchScalarGridSpec(
        num_scalar_prefetch=1,
        grid=(nseq, npages // pp),
        in_specs=_page_specs(pp, 256, 1) + [
            per_seq(DEC_ROWS, 1024), per_seq(2, DEC_ROWS, 128), per_seq(DEC_ROWS, 128), per_seq(DEC_ROWS, 512),
            per_seq(DEC_ROWS, 512), per_seq(DEC_ROWS, 256), per_seq(256, wb),
        ] + _page_specs(pp, 1024, 0) + [per_seq(DEC_ROWS, 512), per_seq(DEC_ROWS, 1024), lspec, lspec, lspec, lspec],
        out_specs=[per_seq(DEC_ROWS, 512), per_seq(DEC_ROWS, 512)],
        scratch_shapes=[pltpu.VMEM((nrow2, 1), F32), pltpu.VMEM((nrow2, 1), F32), pltpu.VMEM((nrow2, V7X_LANES), F32),
                        pltpu.VMEM((nrowd, 1), F32), pltpu.VMEM((nrowd, 1), F32), pltpu.VMEM((nrowd, DIFF_V), F32)],
    )
    return pl.pallas_call(
        functools.partial(_s2_sdiff_kernel, pp=pp, past=past, lam_init=lam_init),
        grid_spec=grid_spec,
        out_shape=[jax.ShapeDtypeStruct((nseq, DEC_ROWS, 512), F32), jax.ShapeDtypeStruct((nseq, DEC_ROWS, 512), F32)],
        compiler_params=_cparams(("parallel", "arbitrary"), 56 << 20),
        name="sample_attention",
    )(page_table, *([cache_nsaT] * pp), qr_s, selfeat, gates_s, oc_s, nsa_new, win_new, state_t,
      *([cache_diff8] * pp), dq_s, diff_new, *lams)


def _post_kernel(x_ref, on_ref, od_ref, p_ref, dg_ref, wo_ref, gpost_ref, gfpre_ref, wup_ref, wdn_ref,
                 gfpost_ref, wple_ref, wgate_ref, gple_ref, y_ref, *, lam_init, ff_chunk):
    x = x_ref[...]
    dg = dg_ref[...] * (1.0 - lam_init)
    parts = [on_ref[...].astype(BF16)]
    for h in range(DIFF_HEADS):
        od = od_ref[:, h * DIFF_V:(h + 1) * DIFF_V]
        od = od * lax.rsqrt(jnp.mean(od * od, axis=-1, keepdims=True) + EPS) * dg
        parts.append(od.astype(BF16))
    cat = jnp.concatenate(parts, axis=1)
    x1 = x + _rms(_dot(cat, wo_ref[...]), gpost_ref[...])
    h1 = _rms(x1, gfpre_ref[...]).astype(BF16)
    f = jnp.zeros(x.shape, F32)
    for c in range(wup_ref.shape[1] // ff_chunk):
        u = jnp.maximum(_dot(h1, wup_ref[:, c * ff_chunk:(c + 1) * ff_chunk]), 0.0)
        f = f + _dot((u * u).astype(BF16), wdn_ref[c * ff_chunk:(c + 1) * ff_chunk, :])
    x2 = x1 + _rms(f, gfpost_ref[...])
    e = _dot(p_ref[...].astype(BF16), wple_ref[...]) * jax.nn.sigmoid(_dot(x2.astype(BF16), wgate_ref[...]))
    y_ref[...] = x2 + _rms(e, gple_ref[...])


def _post(x2d, o_n, o_d, p2d, dgain, wo, gpost, gfpre, wup, wdn, gfpost, wple, wgate, gple, *, lam_init):
    t, d = x2d.shape
    tm = min(TOK_TILE, t)
    assert t % tm == 0
    row = lambda w: pl.BlockSpec((tm, w), lambda i: (i, 0))
    full = lambda a: pl.BlockSpec(a.shape, lambda i: (0,) * a.ndim)
    consts = (dgain, wo, gpost, gfpre, wup, wdn, gfpost, wple, wgate, gple)
    return pl.pallas_call(
        functools.partial(_post_kernel, lam_init=lam_init, ff_chunk=1024),
        grid=(t // tm,),
        in_specs=[row(d), row(512), row(512), row(p2d.shape[1])] + [full(a) for a in consts],
        out_specs=row(d),
        out_shape=jax.ShapeDtypeStruct((t, d), F32),
        compiler_params=_cparams(("parallel",), 54 << 20),
        name="merge_mlp_ple",
    )(x2d, o_n, o_d, p2d, *consts)


def _rope_tables(pos):
    half = HEAD_DIM // 2
    inv = ROPE_THETA ** (-jnp.arange(half, dtype=F32) / half)
    ang = pos.astype(F32)[:, None] * inv[None, :]
    c, s = jnp.cos(ang), jnp.sin(ang)
    return jnp.concatenate([c, c, c, c], axis=-1), jnp.concatenate([-s, s, -s, s], axis=-1)


def _overlap_T(nch):
    cs = np.arange(nch)[None, :] * CMP_STRIDE
    bs = np.arange(128)[:, None] * SEL_BLOCK
    ov = np.clip(np.minimum(cs + CMP_BLOCK, bs + SEL_BLOCK) - np.maximum(cs, bs), 0, None) / CMP_BLOCK
    return ov.astype(np.float32)


def _pair_weights(w1):
    w1r = w1.reshape(CMP_BLOCK, HEAD_DIM, CMP_HIDDEN)
    a = w1r[:CMP_STRIDE].reshape(8, 128, CMP_HIDDEN)
    b = w1r[CMP_STRIDE:].reshape(8, 128, CMP_HIDDEN)
    return jnp.concatenate([a, b], axis=-1)


def _split_weights(w1):
    w1r = w1.reshape(CMP_BLOCK, HEAD_DIM, CMP_HIDDEN)
    ab = jnp.concatenate([w1r[:CMP_STRIDE], w1r[CMP_STRIDE:]], axis=-1)
    z = jnp.zeros_like(ab)
    per_t = jnp.concatenate([jnp.concatenate([ab, z], axis=-1), jnp.concatenate([z, ab], axis=-1)], axis=1)
    return per_t.reshape(CMP_STRIDE // 2, 4 * HEAD_DIM, 4 * CMP_HIDDEN)


def _chunk_perm():
    r = np.arange(128)
    p = np.zeros((128, 128), np.float32)
    p[r, CMP_STRIDE * (r % 8) + r // 8] = 1.0
    return p


def _block_diag2(w2):
    z = jnp.zeros_like(w2)
    return jnp.concatenate([jnp.concatenate([w2, z], axis=1), jnp.concatenate([z, w2], axis=1)], axis=0)


def _token_last(a):
    nd = a.ndim
    t = jnp.transpose(a, (0, 1) + tuple(range(3, nd)) + (2,))
    return t.reshape(a.shape[0], a.shape[1], -1, a.shape[2])


def kernel(x_prompt, x_sample, cache_nsa, cache_diff, state_nsa_win, page_table, p_prompt, p_sample, g_mix_pre, w_in, cmp_pos, cmp_k_w1, cmp_k_w2, cmp_v_w1, cmp_v_w2, diff_lq1, diff_lk1, diff_lq2, diff_lk2, diff_gain, w_out, g_mix_post, g_ffn_pre, w_up, w_down, g_ffn_post, w_ple, w_ple_gate, g_ple):
    b, s, d = x_prompt.shape
    nseq, dec, _ = x_sample.shape
    depth, npool, page = cache_nsa.shape[0], cache_nsa.shape[1], cache_nsa.shape[2]
    npages = page_table.shape[1]
    past = npages * page
    wb = state_nsa_win.shape[2]
    assert page == 128 and dec <= DEC_ROWS and past % SEL_BLOCK == 0 and dec <= SEL_BLOCK
    assert (past + dec - CMP_BLOCK) // CMP_STRIDE + 1 == past // CMP_STRIDE - 1
    assert s % TOK_TILE == 0 and wb == WINDOW and s >= WINDOW

    cos_p, sin_p = _rope_tables(jnp.arange(s))
    pos_s = past + (jnp.arange(nseq * DEC_ROWS) % DEC_ROWS)
    cos_s, sin_s = _rope_tables(pos_s)
    ovlT = jnp.asarray(_overlap_T(s // CMP_STRIDE), BF16)
    ovl_s = jnp.asarray(_overlap_T(past // CMP_STRIDE).T, BF16)
    perm = jnp.asarray(_chunk_perm(), BF16)
    pt = page_table.astype(jnp.int32)
    cache_nsaT = _token_last(cache_nsa).reshape(depth * npool, 4 * NSA_KV * HEAD_DIM, page)
    cache_diff8 = cache_diff.reshape(depth * npool, page * 2 * DIFF_HEADS, DIFF_V)
    state_t = _token_last(state_nsa_win)

    xp = x_prompt.reshape(b * s, d)
    xs = jnp.pad(x_sample, ((0, 0), (0, DEC_ROWS - dec), (0, 0))).reshape(nseq * DEC_ROWS, d)
    outs = [[] for _ in range(6)]
    for i in range(depth):
        lam_init = 0.8 - 0.6 * math.exp(-0.3 * i)
        w = w_in[i]
        wcat = jnp.concatenate([w[:, :1280], w[:, 1304:], w[:, 1280:1304],
                                jnp.zeros((d, _C_END - _C_GATE - 24), w.dtype)], axis=1).astype(BF16)
        wpair = jnp.stack([_pair_weights(cmp_k_w1[i]), _pair_weights(cmp_v_w1[i])]).astype(BF16)
        pospair = jnp.stack([cmp_pos[i][:CMP_STRIDE].reshape(8, 128), cmp_pos[i][CMP_STRIDE:].reshape(8, 128)])
        w2bd = jnp.stack([_block_diag2(cmp_k_w2[i]), _block_diag2(cmp_v_w2[i])]).astype(BF16)
        lams = tuple(a[i].reshape(1, DIFF_QK) for a in (diff_lq1, diff_lk1, diff_lq2, diff_lk2))
        gpre = g_mix_pre[i].reshape(1, d)
        post_w = (diff_gain[i].reshape(1, DIFF_V), w_out[i].astype(BF16), g_mix_post[i].reshape(1, d),
                  g_ffn_pre[i].reshape(1, d), w_up[i].astype(BF16), w_down[i].astype(BF16),
                  g_ffn_post[i].reshape(1, d), w_ple[i].astype(BF16), w_ple_gate[i].astype(BF16),
                  g_ple[i].reshape(1, d))
        pt_i = pt + i * npool

        (qc, qr, dq, gates, nsa_t, win_rows, diff8, cmp_rows,
         kaug, vslcT, kwin, vwinT, dk, dvT) = _project(xp, gpre, wcat, cos_p, sin_p, seq_len=s, prompt=True)
        kc2, vcT = _compress_prompt(cmp_rows, wpair, pospair, w2bd, b=b, s=s)
        o_n = _nsa_prompt(qc, qr, gates, kc2, vcT, ovlT, kaug, vslcT, kwin, vwinT, b=b, s=s)
        o_d = _diff_prompt(dq, dk, dvT, lams, b=b, s=s, lam_init=lam_init)
        xp = _post(xp, o_n, o_d, p_prompt[i].reshape(b * s, -1), *post_w, lam_init=lam_init)
        outs[0].append(jnp.transpose(nsa_t.reshape(b, 4, NSA_KV, HEAD_DIM, s), (0, 4, 1, 2, 3)))
        outs[2].append(diff8.reshape(b, s, 2, DIFF_HEADS, DIFF_V))
        win_tail = lax.optimization_barrier(win_rows.reshape(b, s, 2 * NSA_KV * HEAD_DIM)[:, s - min(WINDOW, s):])
        outs[4].append(win_tail.reshape(b, min(WINDOW, s), 2, NSA_KV, HEAD_DIM))

        qc_s, qr_s, dq_s, gates_s, nsa_new, win_new, diff_new = _project(
            xs, gpre, wcat, cos_s, sin_s, seq_len=None, prompt=False)
        r3 = lambda a: a.reshape(nseq, DEC_ROWS, a.shape[-1])
        wsplit = jnp.stack([_split_weights(cmp_k_w1[i]), _split_weights(cmp_v_w1[i])]).astype(BF16)
        oc_s, selfeat = _s1(pt_i, cache_nsaT, r3(qc_s), wpair, pospair, w2bd, ovl_s, perm, wsplit,
                            nseq=nseq, npages=npages)
        on_s, od_s = _s2_sdiff(pt_i, cache_nsaT, r3(qr_s), selfeat, r3(gates_s), oc_s, r3(nsa_new), r3(win_new),
                               state_t[i], cache_diff8, r3(dq_s), r3(diff_new), lams,
                               nseq=nseq, npages=npages, lam_init=lam_init)
        p_s = jnp.pad(p_sample[i], ((0, 0), (0, DEC_ROWS - dec), (0, 0))).reshape(nseq * DEC_ROWS, -1)
        xs = _post(xs, on_s.reshape(nseq * DEC_ROWS, 512), od_s.reshape(nseq * DEC_ROWS, 512), p_s, *post_w,
                   lam_init=lam_init)
        win_new4 = r3(win_new)[:, :dec].reshape(nseq, dec, 2, NSA_KV, HEAD_DIM)
        outs[1].append(r3(nsa_new)[:, :dec].reshape(nseq, dec, 4, NSA_KV, HEAD_DIM))
        outs[3].append(r3(diff_new)[:, :dec].reshape(nseq, dec, 2, DIFF_HEADS, DIFF_V))
        win_all = jnp.concatenate([state_nsa_win[i], win_new4], axis=1)
        keep = min(WINDOW, past + dec)
        outs[5].append(win_all[:, win_all.shape[1] - keep:])

    y_p = xp.reshape(b, s, d)
    y_s = xs.reshape(nseq, DEC_ROWS, d)[:, :dec]
    return (y_p, y_s) + tuple(jnp.stack(o) for o in outs)
```
